```python
import math
import jax, jax.numpy as jnp
from jax import lax
import numpy as np

D_MODEL = 1024
BATCH = 8
SEQ = 4096
DEPTH = 4

CTX_LEN = 256
GRID_W = 64
D_MIX = D_MODEL
D_CONV = D_MIX // 4
CONV_K = 31
D_SSM = D_MIX // 2
SSM_HEAD_DIM = 64
SSM_HEADS = D_SSM // SSM_HEAD_DIM
SSM_GROUPS = 2
SSM_STATE = 128
SSM_CONV_K = 4
SSM_CHUNK = 64
D_XBC = D_SSM + 2 * SSM_GROUPS * SSM_STATE
D_POOL = D_MIX - D_CONV - D_SSM
POOL_WINDOWS = (2, 4, 8, 16)
POOL_GROUP = D_POOL // len(POOL_WINDOWS)
D_FF = 2816
FFN_K = 3
IN_SPLITS = (D_CONV, 2 * D_CONV, 2 * D_CONV + D_SSM, 2 * D_CONV + D_SSM + D_XBC, 2 * D_CONV + D_SSM + D_XBC + 2 * SSM_HEADS)
IN_COLS = IN_SPLITS[-1] + D_POOL
N_MOD = 6
DEEPNORM_ALPHA = (2.0 * DEPTH) ** 0.25
DEEPNORM_BETA = (8.0 * DEPTH) ** -0.25
LN_EPS = 1e-5
RMS_EPS = 1e-5
DT_MIN = 1e-3
DT_MAX = 1e-1

kernel_name = 'hybrid_conv_ssd_pool_dit_block'

F32 = jnp.float32


def _layer_norm(x, gain, bias):
    xf = x.astype(F32)
    mu = jnp.mean(xf, axis=-1, keepdims=True)
    var = jnp.mean(jnp.square(xf - mu), axis=-1, keepdims=True)
    y = (xf - mu) * lax.rsqrt(var + LN_EPS)
    return (y * gain.astype(F32) + bias.astype(F32)).astype(x.dtype)


def _gated_rms_norm(y, z, gain):
    v = y.astype(F32) * jax.nn.silu(z.astype(F32))
    v = v * lax.rsqrt(jnp.mean(v * v, axis=-1, keepdims=True) + RMS_EPS)
    return (v * gain.astype(F32)).astype(z.dtype)


def _modulate(x, shift, scale):
    return x * (1.0 + scale) + shift


def _dwconv1d(x, w, b, pad):
    y = lax.conv_general_dilated(x, w[:, None, :].astype(x.dtype), window_strides=(1,), padding=[pad],
                                 dimension_numbers=('NWC', 'WIO', 'NWC'), feature_group_count=x.shape[-1])
    return y + b.astype(x.dtype)


def _dwconv2d(x, w, b):
    p = FFN_K // 2
    y = lax.conv_general_dilated(x, w[:, :, None, :].astype(x.dtype), window_strides=(1, 1), padding=[(p, p), (p, p)],
                                 dimension_numbers=('NHWC', 'HWIO', 'NHWC'), feature_group_count=x.shape[-1])
    return y + b.astype(x.dtype)


def _segsum(a):
    T = a.shape[-1]
    a_rep = jnp.broadcast_to(a[..., :, None], a.shape + (T,))
    a_rep = jnp.where(jnp.tril(jnp.ones((T, T), bool), -1), a_rep, 0.0)
    s = jnp.cumsum(a_rep, axis=-2)
    return jnp.where(jnp.tril(jnp.ones((T, T), bool)), s, -jnp.inf)


def _ssd_scan(xh, dt, A, Bm, Cm, h0):
    b, L, H, P = xh.shape
    N = Bm.shape[-1]
    nc, Q = L // SSM_CHUNK, SSM_CHUNK
    xdt = (xh.astype(F32) * dt[..., None]).reshape(b, nc, Q, H, P)
    Bc = Bm.astype(F32).reshape(b, nc, Q, H, N)
    Cc = Cm.astype(F32).reshape(b, nc, Q, H, N)
    a = (dt * A).reshape(b, nc, Q, H).transpose(0, 3, 1, 2)
    a_cum = jnp.cumsum(a, axis=-1)
    scores = jnp.einsum('bclhn,bcshn->bhcls', Cc, Bc) * jnp.exp(_segsum(a))
    y_diag = jnp.einsum('bhcls,bcshp->bclhp', scores, xdt)
    decay_states = jnp.exp(a_cum[..., -1:] - a_cum).transpose(0, 2, 3, 1)
    states = jnp.einsum('bclhn,bclhp->bchpn', Bc * decay_states[..., None], xdt)
    states = jnp.concatenate([h0[:, None], states], axis=1)
    chunk_a = jnp.pad(a_cum[..., -1], ((0, 0), (0, 0), (1, 0)))
    new_states = jnp.einsum('bhzc,bchpn->bzhpn', jnp.exp(_segsum(chunk_a)), states)
    states, final = new_states[:, :-1], new_states[:, -1]
    state_decay = jnp.exp(a_cum).transpose(0, 2, 3, 1)
    y_off = jnp.einsum('bclhn,bchpn->bclhp', Cc, states) * state_decay[..., None]
    return (y_diag + y_off).reshape(b, L, H, P), final


def _ssd_direction(xbc, dt_raw, conv_w, conv_b, dt_bias, A_log, D_skip, h0, reverse):
    if reverse:
        xbc, dt_raw = xbc[:, ::-1], dt_raw[:, ::-1]
    b, L, _ = xbc.shape
    u = jax.nn.silu(_dwconv1d(xbc, conv_w, conv_b, (SSM_CONV_K - 1, 0)))
    xs, Bm, Cm = jnp.split(u, (D_SSM, D_SSM + SSM_GROUPS * SSM_STATE), axis=-1)
    xh = xs.reshape(b, L, SSM_HEADS, SSM_HEAD_DIM)
    rep = SSM_HEADS // SSM_GROUPS
    Bm = jnp.repeat(Bm.reshape(b, L, SSM_GROUPS, SSM_STATE), rep, axis=2)
    Cm = jnp.repeat(Cm.reshape(b, L, SSM_GROUPS, SSM_STATE), rep, axis=2)
    dt = jax.nn.softplus(dt_raw.astype(F32) + dt_bias.astype(F32))
    A = -jnp.exp(A_log.astype(F32))
    y, h_final = _ssd_scan(xh, dt, A, Bm, Cm, h0)
    y = (y + D_skip.astype(F32)[:, None] * xh.astype(F32)).reshape(b, L, D_SSM)
    if reverse:
        y = y[:, ::-1]
    return y, h_final


def _ssd_branch(proj, lp, h0_f, h0_b):
    _, _, z, xbc, dt_raw, _ = jnp.split(proj, IN_SPLITS + (IN_SPLITS[-1],), axis=-1)[:5] + [None]
    y_f, st_f = _ssd_direction(xbc, dt_raw[..., :SSM_HEADS], lp['ssm_conv_w'][0], lp['ssm_conv_b'][0],
                               lp['ssm_dt_bias'][0], lp['ssm_A_log'][0], lp['ssm_D'][0], h0_f, False)
    y_b, st_b = _ssd_direction(xbc, dt_raw[..., SSM_HEADS:], lp['ssm_conv_w'][1], lp['ssm_conv_b'][1],
                               lp['ssm_dt_bias'][1], lp['ssm_A_log'][1], lp['ssm_D'][1], h0_b, True)
    return _gated_rms_norm(y_f + y_b, z, lp['ssm_norm_g']), st_f, st_b


def _conformer_conv(a_val, a_gate, lp):
    u = a_val * jax.nn.sigmoid(a_gate)
    u = _dwconv1d(u, lp['conv_dw_w'], lp['conv_dw_b'], (CONV_K // 2, CONV_K // 2))
    u = jax.nn.silu(_layer_norm(u, lp['conv_ln_g'], lp['conv_ln_b']))
    return u @ lp['conv_pw_w'] + lp['conv_pw_b']


def _pool_mixer(u, lp):
    b, L, _ = u.shape
    uf = u.astype(F32)
    csum = jnp.concatenate([jnp.zeros((b, 1, D_POOL), F32), jnp.cumsum(uf, axis=1)], axis=1)
    t = jnp.arange(L)
    means = []
    for g, w in enumerate(POOL_WINDOWS):
        lo = jnp.clip(t - w // 2, 0, L)
        hi = jnp.clip(t + (w - w // 2), 0, L)
        cs = csum[:, :, g * POOL_GROUP:(g + 1) * POOL_GROUP]
        means.append((cs[:, hi] - cs[:, lo]) / (hi - lo).astype(F32)[None, :, None])
    r = (jnp.concatenate(means, axis=-1) - uf).astype(u.dtype)
    r = jnp.einsum('blgc,gcd->blgd', r.reshape(b, L, len(POOL_WINDOWS), POOL_GROUP), lp['pool_w'])
    return r.reshape(b, L, D_POOL) * lp['pool_scale']


def _mixer(h, lp, h0_f, h0_b):
    proj = h @ lp['w_in']
    a_val, a_gate, _, _, _, u_pool = jnp.split(proj, IN_SPLITS, axis=-1)
    ya = _conformer_conv(a_val, a_gate, lp)
    yb, st_f, st_b = _ssd_branch(proj, lp, h0_f, h0_b)
    yc = _pool_mixer(u_pool, lp)
    return jnp.concatenate([ya, yb, yc], axis=-1) @ lp['w_out'], st_f, st_b


def _conv_ffn(h, lp, rows, width):
    b, L, _ = h.shape
    val, gate = jnp.split(h @ lp['w_up'], 2, axis=-1)
    gate = _dwconv2d(gate.reshape(b, rows, width, D_FF), lp['ffn_dw_w'], lp['ffn_dw_b']).reshape(b, L, D_FF)
    return (val * jax.nn.gelu(gate)) @ lp['w_down']


def setup_inputs(seed: int = 0) -> dict:
    key = jax.random.key(seed)
    ks = iter(jax.random.split(key, 40))

    def nrm(shape, scale):
        return scale * jax.random.normal(next(ks), shape, jnp.float32)

    def near_one(shape):
        return 1.0 + nrm(shape, 0.05)

    L = DEPTH
    dt0 = jnp.exp(jax.random.uniform(next(ks), (L, 2, SSM_HEADS), jnp.float32, math.log(DT_MIN), math.log(DT_MAX)))
    return {
        'x': nrm((BATCH, SEQ, D_MODEL), 1.0),
        'c': nrm((BATCH, D_MODEL), 1.0),
        'ctx': nrm((BATCH, CTX_LEN, D_MODEL), 1.0),
        'c_ctx': nrm((D_MODEL,), 1.0),
        'w_mod': nrm((L, D_MODEL, N_MOD * D_MODEL), 0.5 * D_MODEL ** -0.5),
        'b_mod': nrm((L, N_MOD * D_MODEL), 0.02),
        'w_in': nrm((L, D_MODEL, IN_COLS), D_MODEL ** -0.5),
        'conv_dw_w': nrm((L, CONV_K, D_CONV), CONV_K ** -0.5),
        'conv_dw_b': nrm((L, D_CONV), 0.02),
        'conv_ln_g': near_one((L, D_CONV)),
        'conv_ln_b': nrm((L, D_CONV), 0.02),
        'conv_pw_w': nrm((L, D_CONV, D_CONV), D_CONV ** -0.5),
        'conv_pw_b': nrm((L, D_CONV), 0.02),
        'ssm_conv_w': nrm((L, 2, SSM_CONV_K, D_XBC), SSM_CONV_K ** -0.5),
        'ssm_conv_b': nrm((L, 2, D_XBC), 0.02),
        'ssm_dt_bias': dt0 + jnp.log(-jnp.expm1(-dt0)),
        'ssm_A_log': jnp.log(jax.random.uniform(next(ks), (L, 2, SSM_HEADS), jnp.float32, 1.0, 16.0)),
        'ssm_D': near_one((L, 2, SSM_HEADS)),
        'ssm_norm_g': near_one((L, D_SSM)),
        'pool_w': nrm((L, len(POOL_WINDOWS), POOL_GROUP, POOL_GROUP), POOL_GROUP ** -0.5),
        'pool_scale': near_one((L, D_POOL)),
        'w_out': nrm((L, D_MIX, D_MODEL), DEEPNORM_BETA * D_MIX ** -0.5),
        'ln1_g': near_one((L, D_MODEL)),
        'ln1_b': nrm((L, D_MODEL), 0.02),
        'w_up': nrm((L, D_MODEL, 2 * D_FF), D_MODEL ** -0.5),
        'ffn_dw_w': nrm((L, FFN_K, FFN_K, D_FF), 1.0 / FFN_K),
        'ffn_dw_b': nrm((L, D_FF), 0.02),
        'w_down': nrm((L, D_FF, D_MODEL), DEEPNORM_BETA * D_FF ** -0.5),
        'ln2_g': near_one((L, D_MODEL)),
        'ln2_b': nrm((L, D_MODEL), 0.02),
    }


def reference(x, c, ctx, c_ctx, w_mod, b_mod, w_in, conv_dw_w, conv_dw_b, conv_ln_g, conv_ln_b, conv_pw_w,
              conv_pw_b, ssm_conv_w, ssm_conv_b, ssm_dt_bias, ssm_A_log, ssm_D, ssm_norm_g, pool_w, pool_scale,
              w_out, ln1_g, ln1_b, w_up, ffn_dw_w, ffn_dw_b, w_down, ln2_g, ln2_b):
    rows = x.shape[1] // GRID_W
    ctx_len = ctx.shape[1]
    s_lat = jax.nn.silu(c)
    s_ctx = jax.nn.silu(c_ctx)
    for l in range(DEPTH):
        last = l == DEPTH - 1
        lp = {'w_in': w_in[l], 'conv_dw_w': conv_dw_w[l], 'conv_dw_b': conv_dw_b[l], 'conv_ln_g': conv_ln_g[l],
              'conv_ln_b': conv_ln_b[l], 'conv_pw_w': conv_pw_w[l], 'conv_pw_b': conv_pw_b[l],
              'ssm_conv_w': ssm_conv_w[l], 'ssm_conv_b': ssm_conv_b[l], 'ssm_dt_bias': ssm_dt_bias[l],
              'ssm_A_log': ssm_A_log[l], 'ssm_D': ssm_D[l], 'ssm_norm_g': ssm_norm_g[l], 'pool_w': pool_w[l],
              'pool_scale': pool_scale[l], 'w_out': w_out[l], 'w_up': w_up[l], 'ffn_dw_w': ffn_dw_w[l],
              'ffn_dw_b': ffn_dw_b[l], 'w_down': w_down[l]}
        shx1, scx1, gx1, shx2, scx2, gx2 = jnp.split((s_lat @ w_mod[l] + b_mod[l])[:, None, :], N_MOD, axis=-1)
        shc1, scc1, gc1, shc2, scc2, gc2 = jnp.split(s_ctx @ w_mod[l] + b_mod[l], N_MOD, axis=-1)
        h0 = jnp.zeros((ctx.shape[0], SSM_HEADS, SSM_HEAD_DIM, SSM_STATE), F32)
        h_c = _modulate(ctx, shc1, scc1)
        if last:
            _, st_f, st_b = _ssd_branch(h_c @ lp['w_in'], lp, h0, h0)
        else:
            mix_c, st_f, st_b = _mixer(h_c, lp, h0, h0)
        mix_x, _, _ = _mixer(_modulate(x, shx1, scx1), lp, st_f, st_b)
        x = _layer_norm(DEEPNORM_ALPHA * x + gx1 * mix_x, ln1_g[l], ln1_b[l])
        f_x = _conv_ffn(_modulate(x, shx2, scx2), lp, rows, GRID_W)
        x = _layer_norm(DEEPNORM_ALPHA * x + gx2 * f_x, ln2_g[l], ln2_b[l])
        if not last:
            ctx = _layer_norm(DEEPNORM_ALPHA * ctx + gc1 * mix_c, ln1_g[l], ln1_b[l])
            f_c = _conv_ffn(_modulate(ctx, shc2, scc2), lp, 1, ctx_len)
            ctx = _layer_norm(DEEPNORM_ALPHA * ctx + gc2 * f_c, ln2_g[l], ln2_b[l])
    return x
```

```python
import functools

import numpy as np
import jax
import jax.numpy as jnp
from jax import lax
from jax.experimental import pallas as pl
from jax.experimental.pallas import tpu as pltpu

F32 = jnp.float32
BF16 = jnp.bfloat16

D_MODEL = 1024
DEPTH = 4
GRID_W = 64
D_CONV = 256
CONV_K = 31
D_SSM = 512
HEAD_DIM = 64
HEADS = 8
GROUPS = 2
HEADS_PER_GROUP = HEADS // GROUPS
STATE = 128
SSM_CONV_K = 4
D_XBC = D_SSM + 2 * GROUPS * STATE
D_POOL = 256
POOL_WINDOWS = (2, 4, 8, 16)
POOL_GROUP = D_POOL // len(POOL_WINDOWS)
D_FF = 2816
FFN_K = 3
N_MOD = 6
ALPHA = (2.0 * DEPTH) ** 0.25
LN_EPS = 1e-5
RMS_EPS = 1e-5

LANES = 128
SUBLANES = 8
HALO = 16
DT_PAD = LANES
IN_COLS_PAD = 2 * D_CONV + D_SSM + D_XBC + D_POOL + DT_PAD
SSD_CHUNK = 128
FF_CHUNK = 256
N_FF_CHUNKS = D_FF // FF_CHUNK
MOD_ROWS = 16
MOD_TILE = 1024
VMEM_LIMIT = 52 * 1024 * 1024


def _silu(v):
    return v * jax.nn.sigmoid(v)


def _dot(a, b):
    return jnp.dot(a, b, preferred_element_type=F32)


def _params(sem):
    return pltpu.CompilerParams(dimension_semantics=sem, vmem_limit_bytes=VMEM_LIMIT)


def _mod_kernel(c_ref, w_ref, b_ref, o_ref):
    s = _silu(c_ref[...])
    o_ref[0] = _dot(s.astype(BF16), w_ref[0].astype(BF16)) + b_ref[0]


def _modulation(cvec, w_mod, b_mod):
    n_cols = N_MOD * D_MODEL
    return pl.pallas_call(
        _mod_kernel,
        out_shape=jax.ShapeDtypeStruct((DEPTH, MOD_ROWS, n_cols), F32),
        grid=(DEPTH, n_cols // MOD_TILE),
        in_specs=[
            pl.BlockSpec((MOD_ROWS, D_MODEL), lambda l, j: (0, 0)),
            pl.BlockSpec((1, D_MODEL, MOD_TILE), lambda l, j: (l, 0, j)),
            pl.BlockSpec((1, 1, MOD_TILE), lambda l, j: (l, 0, j)),
        ],
        out_specs=pl.BlockSpec((1, MOD_ROWS, MOD_TILE), lambda l, j: (l, 0, j)),
        compiler_params=_params(("arbitrary", "arbitrary")),
        name="adaln_modulation",
    )(cvec, w_mod, b_mod)


_A0, _Z0, _X0, _P0, _T0 = 0, 2 * D_CONV, 2 * D_CONV + D_SSM, 2 * D_CONV + D_SSM + D_XBC, 2 * D_CONV + D_SSM + D_XBC + D_POOL


def _inproj_kernel(x_ref, sh_ref, sc_ref, w_ref, a_ref, z_ref, xbc_ref, up_ref, dt_ref):
    h = (x_ref[0] * (1.0 + sc_ref[0]) + sh_ref[0]).astype(BF16)
    a_ref[0] = _dot(h, w_ref[:, _A0:_Z0])
    z_ref[0] = _dot(h, w_ref[:, _Z0:_X0])
    xbc_ref[0] = _dot(h, w_ref[:, _X0:_P0])
    up_ref[0] = _dot(h, w_ref[:, _P0:_T0])
    dt_ref[0] = _dot(h, w_ref[:, _T0:IN_COLS_PAD])


def _in_projection(x, shift, scale, w_in, tile):
    B, L, _ = x.shape
    widths = (2 * D_CONV, D_SSM, D_XBC, D_POOL, DT_PAD)
    tok = lambda w: pl.BlockSpec((1, tile, w), lambda b, i: (b, i, 0))
    vec = pl.BlockSpec((1, 1, D_MODEL), lambda b, i: (b, 0, 0))
    return pl.pallas_call(
        _inproj_kernel,
        out_shape=[jax.ShapeDtypeStruct((B, L, w), F32) for w in widths],
        grid=(B, L // tile),
        in_specs=[tok(D_MODEL), vec, vec, pl.BlockSpec((D_MODEL, IN_COLS_PAD), lambda b, i: (0, 0))],
        out_specs=[tok(w) for w in widths],
        compiler_params=_params(("arbitrary", "arbitrary")),
        name="in_projection",
    )(x, shift, scale, w_in)


def _split_bf16(v, parts):
    out = []
    for _ in range(parts - 1):
        p = v.astype(BF16)
        out.append(p)
        v = v - p.astype(F32)
    out.append(v.astype(BF16))
    return out


def _ssd_kernel(xbc_ref, dt_ref, h0_ref, cw_ref, cb_ref, dtb_ref, alog_ref, dskip_ref, tri_ref, expand_ref,
                y_ref, st_ref, ext_ref, *, reverse, direction):
    Q = SSD_CHUNK
    i = pl.program_id(1)

    @pl.when(i == 0)
    def _():
        st_ref[...] = h0_ref[...]
        ext_ref[...] = jnp.zeros_like(ext_ref)

    xbc = xbc_ref[0]
    conv = cb_ref[...]
    if not reverse:
        ext_ref[SUBLANES:SUBLANES + Q, :] = xbc
        for k in range(SSM_CONV_K):
            off = SUBLANES - (SSM_CONV_K - 1) + k
            conv = conv + cw_ref[k:k + 1, :] * ext_ref[off:off + Q, :]
        ext_ref[0:SUBLANES, :] = xbc[Q - SUBLANES:Q, :]
    else:
        ext_ref[0:Q, :] = xbc
        for k in range(SSM_CONV_K):
            off = SSM_CONV_K - 1 - k
            conv = conv + cw_ref[k:k + 1, :] * ext_ref[off:off + Q, :]
        ext_ref[Q:Q + SUBLANES, :] = xbc[0:SUBLANES, :]
    u = _silu(conv)
    xs = u[:, :D_SSM]
    b_bf = u[:, D_SSM:D_SSM + GROUPS * STATE].astype(BF16)
    c_bf = u[:, D_SSM + GROUPS * STATE:].astype(BF16)

    lane = lax.broadcasted_iota(jnp.int32, (1, LANES), 1)
    own = (lane >= direction * HEADS) & (lane < (direction + 1) * HEADS)
    a_head = jnp.where(own, -jnp.exp(alog_ref[...]), 0.0)
    dt = jax.nn.softplus(dt_ref[0] + dtb_ref[...])
    a = dt * a_head
    acum = _dot(tri_ref[...], jnp.concatenate(_split_bf16(a, 3), axis=0))
    acum_t = acum.T
    last = Q - 1 if not reverse else 0
    total = acum[last:last + 1, :]
    stacked = jnp.concatenate([dt, jnp.exp(acum), jnp.exp(total - acum)], axis=0)
    expanded = _dot(jnp.concatenate(_split_bf16(stacked, 2), axis=1), expand_ref[...])
    dt_x, decay_in, decay_out = expanded[0:Q], expanded[Q:2 * Q], expanded[2 * Q:3 * Q]

    xdt = xs * dt_x
    xdt_bf = xdt.astype(BF16)
    xdec_bf = (xdt * decay_out).astype(BF16)
    row = lax.broadcasted_iota(jnp.int32, (Q, Q), 0)
    col = lax.broadcasted_iota(jnp.int32, (Q, Q), 1)
    mask = (row >= col) if not reverse else (row <= col)
    gw = HEADS_PER_GROUP * HEAD_DIM
    for g in range(GROUPS):
        cg = c_bf[:, g * STATE:(g + 1) * STATE]
        bg = b_bf[:, g * STATE:(g + 1) * STATE]
        scores = lax.dot_general(cg, bg, (((1,), (1,)), ((), ())), preferred_element_type=F32)
        state = st_ref[0, g]
        y_off = _dot(cg, state.astype(BF16))
        heads = []
        for hh in range(HEADS_PER_GROUP):
            h = g * HEADS_PER_GROUP + hh
            ln = direction * HEADS + h
            seg = jnp.where(mask, jnp.exp(acum[:, ln:ln + 1] - acum_t[ln:ln + 1, :]), 0.0)
            heads.append(_dot((scores * seg).astype(BF16), xdt_bf[:, h * HEAD_DIM:(h + 1) * HEAD_DIM]))
        y_diag = jnp.concatenate(heads, axis=1)
        sl = slice(g * gw, (g + 1) * gw)
        y_ref[0, :, sl] = y_diag + y_off * decay_in[:, sl] + dskip_ref[:, sl] * xs[:, sl]
        upd = lax.dot_general(bg, xdec_bf[:, sl], (((0,), (0,)), ((), ())), preferred_element_type=F32)
        st_ref[0, g] = state * decay_in[last:last + 1, sl] + upd


def _ssd_scan(xbc, dt, h0, lp, direction):
    B, L, _ = xbc.shape
    Q = SSD_CHUNK
    n_chunks = L // Q
    reverse = direction == 1
    chunk = (lambda i: n_chunks - 1 - i) if reverse else (lambda i: i)
    tri = np.tril(np.ones((Q, Q), np.float32))
    if reverse:
        tri = tri.T
    tri3 = jnp.asarray(np.concatenate([tri, tri, tri], axis=1), BF16)
    expand = np.zeros((LANES, D_SSM), np.float32)
    for h in range(HEADS):
        expand[direction * HEADS + h, h * HEAD_DIM:(h + 1) * HEAD_DIM] = 1.0
    expand2 = jnp.asarray(np.concatenate([expand, expand], axis=0), BF16)
    gw = HEADS_PER_GROUP * HEAD_DIM
    const = lambda shape: pl.BlockSpec(shape, lambda b, i: (0,) * len(shape))
    state_spec = pl.BlockSpec((1, GROUPS, STATE, gw), lambda b, i: (b, 0, 0, 0))
    return pl.pallas_call(
        functools.partial(_ssd_kernel, reverse=reverse, direction=direction),
        out_shape=[jax.ShapeDtypeStruct((B, L, D_SSM), F32), jax.ShapeDtypeStruct((B, GROUPS, STATE, gw), F32)],
        grid=(B, n_chunks),
        in_specs=[
            pl.BlockSpec((1, Q, D_XBC), lambda b, i: (b, chunk(i), 0)),
            pl.BlockSpec((1, Q, DT_PAD), lambda b, i: (b, chunk(i), 0)),
            state_spec,
            const((SSM_CONV_K, D_XBC)), const((1, D_XBC)), const((1, DT_PAD)), const((1, DT_PAD)),
            const((1, D_SSM)), const((Q, 3 * Q)), const((2 * LANES, D_SSM)),
        ],
        out_specs=[pl.BlockSpec((1, Q, D_SSM), lambda b, i: (b, chunk(i), 0)), state_spec],
        scratch_shapes=[pltpu.VMEM((Q + 2 * SUBLANES, D_XBC), F32)],
        compiler_params=_params(("arbitrary", "arbitrary")),
        name="ssd_scan_bwd" if reverse else "ssd_scan_fwd",
    )(xbc, dt, h0, lp["ssm_conv_w"][direction], lp["ssm_conv_b"][direction], lp["dt_bias"], lp["a_log"],
      lp["d_skip"][direction], tri3, expand2)


def _layer_norm(v, gain, bias):
    mu = jnp.mean(v, axis=-1, keepdims=True)
    d = v - mu
    var = jnp.mean(d * d, axis=-1, keepdims=True)
    return d * lax.rsqrt(var + LN_EPS) * gain + bias


MIX_ROWS = 64


def _mix_kernel(a_ref, ap_ref, an_ref, up_ref, upp_ref, upn_ref, yf_ref, yb_ref, z_ref, x_ref, gate_ref,
                dww_ref, dwb_ref, clg_ref, clb_ref, pww_ref, pwb_ref, ng_ref, plw_ref, pls_ref, wo_ref,
                l1g_ref, l1b_ref, o_ref, ext_ref, pext_ref, conv_ref, pool_ref, *, tile, seq_len):
    i = pl.program_id(1)
    n_tiles = seq_len // tile
    has_prev = jnp.where(i > 0, 1.0, 0.0)
    has_next = jnp.where(i < n_tiles - 1, 1.0, 0.0)

    def glu(v):
        return v[:, :D_CONV] * jax.nn.sigmoid(v[:, D_CONV:])

    ext_ref[0:HALO, :] = glu(ap_ref[0]) * has_prev
    ext_ref[HALO:HALO + tile, :] = glu(a_ref[0])
    ext_ref[HALO + tile:, :] = glu(an_ref[0]) * has_next
    pext_ref[0:HALO, :] = upp_ref[0] * has_prev
    pext_ref[HALO:HALO + tile, :] = up_ref[0]
    pext_ref[HALO + tile:, :] = upn_ref[0] * has_next

    lane = lax.broadcasted_iota(jnp.int32, (MIX_ROWS, D_POOL), 1)
    group = lane // POOL_GROUP
    row = lax.broadcasted_iota(jnp.int32, (MIX_ROWS, D_POOL), 0)
    half_c = CONV_K // 2
    half_p = POOL_WINDOWS[-1] // 2
    for r in range(tile // MIX_ROWS):
        base = r * MIX_ROWS
        acc = dwb_ref[...] + dww_ref[0:1, :] * ext_ref[base + HALO - half_c:base + HALO - half_c + MIX_ROWS, :]
        for k in range(1, CONV_K):
            o = base + HALO - half_c + k
            acc = acc + dww_ref[k:k + 1, :] * ext_ref[o:o + MIX_ROWS, :]
        conv_ref[base:base + MIX_ROWS, :] = acc
        c = base + HALO
        sums = []
        s = pext_ref[c - 1:c - 1 + MIX_ROWS, :] + pext_ref[c:c + MIX_ROWS, :]
        sums.append(s)
        lo, hi = 1, 0
        for w in POOL_WINDOWS[1:]:
            for j in range(w // 2, lo, -1):
                s = s + pext_ref[c - j:c - j + MIX_ROWS, :]
            for j in range(hi + 1, w - w // 2):
                s = s + pext_ref[c + j:c + j + MIX_ROWS, :]
            lo, hi = w // 2, w - w // 2 - 1
            sums.append(s)
        t = row + (i * tile + base)
        mean = jnp.zeros((MIX_ROWS, D_POOL), F32)
        for gi, w in enumerate(POOL_WINDOWS):
            cnt = jnp.minimum(t + (w - w // 2), seq_len) - jnp.maximum(t - w // 2, 0)
            mean = jnp.where(group == gi, sums[gi] / cnt.astype(F32), mean)
        pool_ref[base:base + MIX_ROWS, :] = mean - pext_ref[c:c + MIX_ROWS, :]
    del half_p

    ya = _silu(_layer_norm(conv_ref[...], clg_ref[...], clb_ref[...]))
    ya = _dot(ya.astype(BF16), pww_ref[...]) + pwb_ref[...]
    yc = _dot(pool_ref[...].astype(BF16), plw_ref[...]) * pls_ref[...]
    v = (yf_ref[0] + yb_ref[0]) * _silu(z_ref[0])
    yb = v * lax.rsqrt(jnp.mean(v * v, axis=-1, keepdims=True) + RMS_EPS) * ng_ref[...]
    mix = (_dot(ya.astype(BF16), wo_ref[0:D_CONV, :])
           + _dot(yb.astype(BF16), wo_ref[D_CONV:D_CONV + D_SSM, :])
           + _dot(yc.astype(BF16), wo_ref[D_CONV + D_SSM:, :]))
    o_ref[0] = _layer_norm(ALPHA * x_ref[0] + gate_ref[0] * mix, l1g_ref[...], l1b_ref[...])


def _mixers(a, up, y_f, y_b, z, x, gate, lp, tile):
    B, L, _ = x.shape
    hb = tile // HALO
    n_halo = L // HALO
    tok = lambda w: pl.BlockSpec((1, tile, w), lambda b, i: (b, i, 0))
    prev = lambda w: pl.BlockSpec((1, HALO, w), lambda b, i: (b, jnp.maximum(i * hb - 1, 0), 0))
    nxt = lambda w: pl.BlockSpec((1, HALO, w), lambda b, i: (b, jnp.minimum((i + 1) * hb, n_halo - 1), 0))
    const = lambda shape: pl.BlockSpec(shape, lambda b, i: (0,) * len(shape))
    return pl.pallas_call(
        functools.partial(_mix_kernel, tile=tile, seq_len=L),
        out_shape=jax.ShapeDtypeStruct((B, L, D_MODEL), F32),
        grid=(B, L // tile),
        in_specs=[
            tok(2 * D_CONV), prev(2 * D_CONV), nxt(2 * D_CONV),
            tok(D_POOL), prev(D_POOL), nxt(D_POOL),
            tok(D_SSM), tok(D_SSM), tok(D_SSM), tok(D_MODEL),
            pl.BlockSpec((1, 1, D_MODEL), lambda b, i: (b, 0, 0)),
            const((CONV_K, D_CONV)), const((1, D_CONV)), const((1, D_CONV)), const((1, D_CONV)),
            const((D_CONV, D_CONV)), const((1, D_CONV)), const((1, D_SSM)),
            const((D_POOL, D_POOL)), const((1, D_POOL)), const((D_MODEL, D_MODEL)),
            const((1, D_MODEL)), const((1, D_MODEL)),
        ],
        out_specs=tok(D_MODEL),
        scratch_shapes=[pltpu.VMEM((tile + 2 * HALO, D_CONV), F32), pltpu.VMEM((tile + 2 * HALO, D_POOL), F32),
                        pltpu.VMEM((tile, D_CONV), F32), pltpu.VMEM((tile, D_POOL), F32)],
        compiler_params=_params(("arbitrary", "arbitrary")),
        name="mixers_outproj_ln1",
    )(a, a, a, up, up, up, y_f, y_b, z, x, gate,
      lp["conv_dw_w"], lp["conv_dw_b"], lp["conv_ln_g"], lp["conv_ln_b"], lp["conv_pw_w"], lp["conv_pw_b"],
      lp["ssm_norm_g"], lp["pool_w"], lp["pool_scale"], lp["w_out"], lp["ln1_g"], lp["ln1_b"])


def _ffn_kernel(*refs, rows, width, has_halo):
    if has_halo:
        (x_ref, xp_ref, xn_ref, sh_ref, sc_ref, gate_ref, wv_ref, wg_ref, wd_ref, dww_ref, dwb_ref, l2g_ref, l2b_ref,
         o_ref, h_ref, g0_ref, gm_ref, gp_ref, val_ref, act_ref, acc_ref) = refs
    else:
        (x_ref, sh_ref, sc_ref, gate_ref, wv_ref, wg_ref, wd_ref, dww_ref, dwb_ref, l2g_ref, l2b_ref,
         o_ref, h_ref, g0_ref, gm_ref, gp_ref, val_ref, act_ref, acc_ref) = refs
    tile = rows * width
    ext_rows = rows + 2 if has_halo else rows
    first = width if has_halo else 0
    i = pl.program_id(1)
    n_tiles = pl.num_programs(1)

    def modulate(v):
        return (v * (1.0 + sc_ref[0]) + sh_ref[0]).astype(BF16)

    h_ref[first:first + tile, :] = modulate(x_ref[0])
    if has_halo:
        h_ref[0:width, :] = modulate(xp_ref[0])
        h_ref[first + tile:, :] = modulate(xn_ref[0])
        has_prev = jnp.where(i > 0, 1.0, 0.0)
        has_next = jnp.where(i < n_tiles - 1, 1.0, 0.0)
    acc_ref[...] = jnp.zeros_like(acc_ref)
    tok = lax.broadcasted_iota(jnp.int32, (width, FF_CHUNK), 0)
    not_first = tok != 0
    not_last = tok != width - 1

    def step(j, carry):
        g0_ref[...] = _dot(h_ref[...], wg_ref[j])
        if has_halo:
            g0_ref[0:width, :] = g0_ref[0:width, :] * has_prev
            g0_ref[first + tile:, :] = g0_ref[first + tile:, :] * has_next
        val_ref[...] = _dot(h_ref[first:first + tile, :], wv_ref[j])
        for e in range(ext_rows):
            blk = g0_ref[e * width:(e + 1) * width, :]
            gm_ref[e * width:(e + 1) * width, :] = jnp.where(not_first, pltpu.roll(blk, 1, 0), 0.0)
            gp_ref[e * width:(e + 1) * width, :] = jnp.where(not_last, pltpu.roll(blk, width - 1, 0), 0.0)
        tap = lambda k: dww_ref[j, k:k + 1, :]
        for r in range(rows):
            conv = dwb_ref[j]
            for dr in range(FFN_K):
                e = r + dr - 1 + (1 if has_halo else 0)
                if e < 0 or e >= ext_rows:
                    continue
                sl = slice(e * width, (e + 1) * width)
                conv = (conv + tap(3 * dr) * gm_ref[sl, :] + tap(3 * dr + 1) * g0_ref[sl, :]
                        + tap(3 * dr + 2) * gp_ref[sl, :])
            out = slice(r * width, (r + 1) * width)
            act_ref[out, :] = (val_ref[out, :] * jax.nn.gelu(conv)).astype(BF16)
        acc_ref[...] += _dot(act_ref[...], wd_ref[j])
        return carry

    lax.fori_loop(0, N_FF_CHUNKS, step, 0)
    o_ref[0] = _layer_norm(ALPHA * x_ref[0] + gate_ref[0] * acc_ref[...], l2g_ref[...], l2b_ref[...])


def _conv_ffn(x, shift, scale, gate, lp, rows, width):
    B, L, _ = x.shape
    tile = rows * width
    n_tiles = L // tile
    has_halo = n_tiles * rows > 1
    n_rows = L // width
    ext = (rows + 2) * width if has_halo else tile
    tok = pl.BlockSpec((1, tile, D_MODEL), lambda b, i: (b, i, 0))
    vec = pl.BlockSpec((1, 1, D_MODEL), lambda b, i: (b, 0, 0))
    const = lambda shape: pl.BlockSpec(shape, lambda b, i: (0,) * len(shape), pipeline_mode=pl.Buffered(1))
    x_specs, x_args = [tok], [x]
    if has_halo:
        x_specs += [pl.BlockSpec((1, width, D_MODEL), lambda b, i: (b, jnp.maximum(i * rows - 1, 0), 0)),
                    pl.BlockSpec((1, width, D_MODEL), lambda b, i: (b, jnp.minimum((i + 1) * rows, n_rows - 1), 0))]
        x_args += [x, x]
    return pl.pallas_call(
        functools.partial(_ffn_kernel, rows=rows, width=width, has_halo=has_halo),
        out_shape=jax.ShapeDtypeStruct((B, L, D_MODEL), F32),
        grid=(B, n_tiles),
        in_specs=x_specs + [
            vec, vec, vec,
            const((N_FF_CHUNKS, D_MODEL, FF_CHUNK)), const((N_FF_CHUNKS, D_MODEL, FF_CHUNK)),
            const((N_FF_CHUNKS, FF_CHUNK, D_MODEL)),
            const((N_FF_CHUNKS, FFN_K * FFN_K, FF_CHUNK)), const((N_FF_CHUNKS, 1, FF_CHUNK)),
            const((1, D_MODEL)), const((1, D_MODEL)),
        ],
        out_specs=tok,
        scratch_shapes=[pltpu.VMEM((ext, D_MODEL), BF16), pltpu.VMEM((ext, FF_CHUNK), F32),
                        pltpu.VMEM((ext, FF_CHUNK), F32), pltpu.VMEM((ext, FF_CHUNK), F32),
                        pltpu.VMEM((tile, FF_CHUNK), F32), pltpu.VMEM((tile, FF_CHUNK), BF16),
                        pltpu.VMEM((tile, D_MODEL), F32)],
        compiler_params=_params(("arbitrary", "arbitrary")),
        name="conv_ffn_ln2",
    )(*x_args, shift, scale, gate, lp["w_up_val"], lp["w_up_gate"], lp["w_down"], lp["ffn_dw_w"], lp["ffn_dw_b"],
      lp["ln2_g"], lp["ln2_b"])


def _layer_params(l, w_in, conv_dw_w, conv_dw_b, conv_ln_g, conv_ln_b, conv_pw_w, conv_pw_b, ssm_conv_w, ssm_conv_b,
                  ssm_dt_bias, ssm_A_log, ssm_D, ssm_norm_g, pool_w, pool_scale, w_out, ln1_g, ln1_b, w_up, ffn_dw_w,
                  ffn_dw_b, w_down, ln2_g, ln2_b):
    row = lambda v: v.reshape(1, -1)
    wi = w_in[l]
    s0, s1, s2, s3, s4 = 2 * D_CONV, 2 * D_CONV + D_SSM, 2 * D_CONV + D_SSM + D_XBC, 2 * D_CONV + D_SSM + D_XBC + 2 * HEADS, wi.shape[1]
    w_in_r = jnp.concatenate([wi[:, :s2], wi[:, s3:s4], wi[:, s2:s3], jnp.zeros((D_MODEL, DT_PAD - 2 * HEADS), F32)],
                             axis=1).astype(BF16)
    pad_lanes = lambda v: jnp.pad(v.reshape(1, -1), ((0, 0), (0, DT_PAD - 2 * HEADS)))
    pool_bd = jnp.zeros((D_POOL, D_POOL), F32)
    for g in range(len(POOL_WINDOWS)):
        sl = slice(g * POOL_GROUP, (g + 1) * POOL_GROUP)
        pool_bd = pool_bd.at[sl, sl].set(pool_w[l, g])
    chunks = lambda v: v.reshape(v.shape[0], N_FF_CHUNKS, FF_CHUNK).transpose(1, 0, 2)
    return {
        "w_in": w_in_r,
        "conv_dw_w": conv_dw_w[l], "conv_dw_b": row(conv_dw_b[l]), "conv_ln_g": row(conv_ln_g[l]),
        "conv_ln_b": row(conv_ln_b[l]), "conv_pw_w": conv_pw_w[l].astype(BF16), "conv_pw_b": row(conv_pw_b[l]),
        "ssm_conv_w": ssm_conv_w[l], "ssm_conv_b": ssm_conv_b[l].reshape(2, 1, D_XBC),
        "dt_bias": pad_lanes(ssm_dt_bias[l]), "a_log": pad_lanes(ssm_A_log[l]),
        "d_skip": jnp.repeat(ssm_D[l], HEAD_DIM, axis=1).reshape(2, 1, D_SSM),
        "ssm_norm_g": row(ssm_norm_g[l]), "pool_w": pool_bd.astype(BF16), "pool_scale": row(pool_scale[l]),
        "w_out": w_out[l].astype(BF16), "ln1_g": row(ln1_g[l]), "ln1_b": row(ln1_b[l]),
        "w_up_val": chunks(w_up[l, :, :D_FF]).astype(BF16), "w_up_gate": chunks(w_up[l, :, D_FF:]).astype(BF16),
        "w_down": w_down[l].reshape(N_FF_CHUNKS, FF_CHUNK, D_MODEL).astype(BF16),
        "ffn_dw_w": chunks(ffn_dw_w[l].reshape(FFN_K * FFN_K, D_FF)), "ffn_dw_b": chunks(ffn_dw_b[l].reshape(1, D_FF)),
        "ln2_g": row(ln2_g[l]), "ln2_b": row(ln2_b[l]),
    }


def _ssd_branch(h, shift, scale, lp, h0_f, h0_b, tile):
    a, z, xbc, up, dt = _in_projection(h, shift, scale, lp["w_in"], tile)
    y_f, st_f = _ssd_scan(xbc, dt, h0_f, lp, 0)
    y_b, st_b = _ssd_scan(xbc, dt, h0_b, lp, 1)
    return (a, up, y_f, y_b, z), st_f, st_b


def kernel(x, c, ctx, c_ctx, w_mod, b_mod, w_in, conv_dw_w, conv_dw_b, conv_ln_g, conv_ln_b, conv_pw_w, conv_pw_b,
           ssm_conv_w, ssm_conv_b, ssm_dt_bias, ssm_A_log, ssm_D, ssm_norm_g, pool_w, pool_scale, w_out, ln1_g, ln1_b,
           w_up, ffn_dw_w, ffn_dw_b, w_down, ln2_g, ln2_b):
    B, L, _ = x.shape
    ctx_len = ctx.shape[1]
    assert L % (8 * GRID_W) == 0 and ctx_len % SSD_CHUNK == 0 and B + 1 <= MOD_ROWS
    cvec = jnp.concatenate([c, c_ctx[None, :], jnp.zeros((MOD_ROWS - B - 1, D_MODEL), F32)], axis=0)
    mods = _modulation(cvec, w_mod, b_mod.reshape(DEPTH, 1, N_MOD * D_MODEL))
    zero_state = jnp.zeros((B, GROUPS, STATE, HEADS_PER_GROUP * HEAD_DIM), F32)
    lat_tile = 8 * GRID_W
    for l in range(DEPTH):
        last = l == DEPTH - 1
        lp = _layer_params(l, w_in, conv_dw_w, conv_dw_b, conv_ln_g, conv_ln_b, conv_pw_w, conv_pw_b, ssm_conv_w,
                           ssm_conv_b, ssm_dt_bias, ssm_A_log, ssm_D, ssm_norm_g, pool_w, pool_scale, w_out, ln1_g,
                           ln1_b, w_up, ffn_dw_w, ffn_dw_b, w_down, ln2_g, ln2_b)
        mx = [mods[l, :B, k * D_MODEL:(k + 1) * D_MODEL].reshape(B, 1, D_MODEL) for k in range(N_MOD)]
        mc = [jnp.broadcast_to(mods[l, B, k * D_MODEL:(k + 1) * D_MODEL].reshape(1, 1, D_MODEL), (B, 1, D_MODEL))
              for k in range(N_MOD)]
        (a, up, y_f, y_b, z), st_f, st_b = _ssd_branch(ctx, mc[0], mc[1], lp, zero_state, zero_state, ctx_len)
        if not last:
            ctx1 = _mixers(a, up, y_f, y_b, z, ctx, mc[2], lp, ctx_len)
            ctx = _conv_ffn(ctx1, mc[3], mc[4], mc[5], lp, 1, ctx_len)
        (a, up, y_f, y_b, z), _, _ = _ssd_branch(x, mx[0], mx[1], lp, st_f, st_b, lat_tile)
        x1 = _mixers(a, up, y_f, y_b, z, x, mx[2], lp, lat_tile)
        x = _conv_ffn(x1, mx[3], mx[4], mx[5], lp, 8, GRID_W)
    return x
```

```python
import functools

import numpy as np
import jax
import jax.numpy as jnp
from jax import lax
from jax.experimental import pallas as pl
from jax.experimental.pallas import tpu as pltpu

F32 = jnp.float32
BF16 = jnp.bfloat16

D_MODEL = 1024
DEPTH = 4
GRID_W = 64
D_CONV = 256
CONV_K = 31
D_SSM = 512
HEAD_DIM = 64
HEADS = 8
GROUPS = 2
HEADS_PER_GROUP = HEADS // GROUPS
STATE = 128
SSM_CONV_K = 4
D_XBC = D_SSM + 2 * GROUPS * STATE
D_POOL = 256
POOL_WINDOWS = (2, 4, 8, 16)
POOL_GROUP = D_POOL // len(POOL_WINDOWS)
D_FF = 2816
FFN_K = 3
N_MOD = 6
ALPHA = (2.0 * DEPTH) ** 0.25
LN_EPS = 1e-5
RMS_EPS = 1e-5

LANES = 128
SUBLANES = 8
HALO = 16
DT_PAD = LANES
IN_COLS_PAD = 2 * D_CONV + D_SSM + D_XBC + D_POOL + DT_PAD
SSD_CHUNK = 128
SSD_BLOCK = 512
FF_CHUNK = 256
N_FF_CHUNKS = D_FF // FF_CHUNK
MOD_ROWS = 16
MOD_TILE = 1024
VMEM_LIMIT = 52 * 1024 * 1024


def _silu(v):
    return v * jax.nn.sigmoid(v)


def _dot(a, b):
    return jnp.dot(a, b, preferred_element_type=F32)


def _params(sem):
    return pltpu.CompilerParams(dimension_semantics=sem, vmem_limit_bytes=VMEM_LIMIT)


def _mod_kernel(c_ref, w_ref, b_ref, o_ref):
    s = _silu(c_ref[...])
    o_ref[0] = _dot(s.astype(BF16), w_ref[0].astype(BF16)) + b_ref[0]


def _modulation(cvec, w_mod, b_mod):
    n_cols = N_MOD * D_MODEL
    return pl.pallas_call(
        _mod_kernel,
        out_shape=jax.ShapeDtypeStruct((DEPTH, MOD_ROWS, n_cols), F32),
        grid=(DEPTH, n_cols // MOD_TILE),
        in_specs=[
            pl.BlockSpec((MOD_ROWS, D_MODEL), lambda l, j: (0, 0)),
            pl.BlockSpec((1, D_MODEL, MOD_TILE), lambda l, j: (l, 0, j)),
            pl.BlockSpec((1, 1, MOD_TILE), lambda l, j: (l, 0, j)),
        ],
        out_specs=pl.BlockSpec((1, MOD_ROWS, MOD_TILE), lambda l, j: (l, 0, j)),
        compiler_params=_params(("arbitrary", "arbitrary")),
        name="adaln_modulation",
    )(cvec, w_mod, b_mod)


_A0, _Z0, _X0, _P0, _T0 = 0, 2 * D_CONV, 2 * D_CONV + D_SSM, 2 * D_CONV + D_SSM + D_XBC, 2 * D_CONV + D_SSM + D_XBC + D_POOL


def _inproj_kernel(x_ref, sh_ref, sc_ref, w_ref, a_ref, z_ref, xbc_ref, up_ref, dt_ref):
    h = (x_ref[0] * (1.0 + sc_ref[0]) + sh_ref[0]).astype(BF16)
    a_ref[0] = _dot(h, w_ref[:, _A0:_Z0])
    z_ref[0] = _dot(h, w_ref[:, _Z0:_X0])
    xbc_ref[0] = _dot(h, w_ref[:, _X0:_P0])
    up_ref[0] = _dot(h, w_ref[:, _P0:_T0])
    dt_ref[0] = _dot(h, w_ref[:, _T0:IN_COLS_PAD])


def _in_projection(x, shift, scale, w_in, tile):
    B, L, _ = x.shape
    widths = (2 * D_CONV, D_SSM, D_XBC, D_POOL, DT_PAD)
    tok = lambda w: pl.BlockSpec((1, tile, w), lambda b, i: (b, i, 0))
    vec = pl.BlockSpec((1, 1, D_MODEL), lambda b, i: (b, 0, 0))
    return pl.pallas_call(
        _inproj_kernel,
        out_shape=[jax.ShapeDtypeStruct((B, L, w), F32) for w in widths],
        grid=(B, L // tile),
        in_specs=[tok(D_MODEL), vec, vec, pl.BlockSpec((D_MODEL, IN_COLS_PAD), lambda b, i: (0, 0))],
        out_specs=[tok(w) for w in widths],
        compiler_params=_params(("arbitrary", "arbitrary")),
        name="in_projection",
    )(x, shift, scale, w_in)


def _split_bf16(v, parts):
    out = []
    for _ in range(parts - 1):
        p = v.astype(BF16)
        out.append(p)
        v = v - p.astype(F32)
    out.append(v.astype(BF16))
    return out


def _ssd_kernel(xbc_ref, dt_ref, h0_ref, cw_ref, cb_ref, dtb_ref, alog_ref, dskip_ref, tri_ref, expand_ref,
                y_ref, st_ref, ext_ref, *, reverse, direction):
    i = pl.program_id(1)

    @pl.when(i == 0)
    def _():
        st_ref[...] = h0_ref[...]
        ext_ref[...] = jnp.zeros_like(ext_ref)

    n_sub = xbc_ref.shape[1] // SSD_CHUNK
    for sub in (range(n_sub - 1, -1, -1) if reverse else range(n_sub)):
        _ssd_chunk(xbc_ref, dt_ref, cw_ref, cb_ref, dtb_ref, alog_ref, dskip_ref, tri_ref, expand_ref, y_ref, st_ref,
                   ext_ref, sub * SSD_CHUNK, reverse, direction)


def _ssd_chunk(xbc_ref, dt_ref, cw_ref, cb_ref, dtb_ref, alog_ref, dskip_ref, tri_ref, expand_ref, y_ref, st_ref,
               ext_ref, r0, reverse, direction):
    Q = SSD_CHUNK
    rows = slice(r0, r0 + Q)
    xbc = xbc_ref[0, rows, :]
    conv = cb_ref[...]
    if not reverse:
        ext_ref[SUBLANES:SUBLANES + Q, :] = xbc
        for k in range(SSM_CONV_K):
            off = SUBLANES - (SSM_CONV_K - 1) + k
            conv = conv + cw_ref[k:k + 1, :] * ext_ref[off:off + Q, :]
        ext_ref[0:SUBLANES, :] = xbc[Q - SUBLANES:Q, :]
    else:
        ext_ref[0:Q, :] = xbc
        for k in range(SSM_CONV_K):
            off = SSM_CONV_K - 1 - k
            conv = conv + cw_ref[k:k + 1, :] * ext_ref[off:off + Q, :]
        ext_ref[Q:Q + SUBLANES, :] = xbc[0:SUBLANES, :]
    u = _silu(conv)
    xs = u[:, :D_SSM]
    b_bf = u[:, D_SSM:D_SSM + GROUPS * STATE].astype(BF16)
    c_bf = u[:, D_SSM + GROUPS * STATE:].astype(BF16)

    lane = lax.broadcasted_iota(jnp.int32, (1, LANES), 1)
    own = (lane >= direction * HEADS) & (lane < (direction + 1) * HEADS)
    a_head = jnp.where(own, -jnp.exp(alog_ref[...]), 0.0)
    dt = jax.nn.softplus(dt_ref[0, rows, :] + dtb_ref[...])
    a = dt * a_head
    acum = _dot(tri_ref[...], jnp.concatenate(_split_bf16(a, 3), axis=0))
    acum_t = acum.T
    last = Q - 1 if not reverse else 0
    total = acum[last:last + 1, :]
    stacked = jnp.concatenate([dt, jnp.exp(acum), jnp.exp(total - acum)], axis=0)
    expanded = _dot(jnp.concatenate(_split_bf16(stacked, 2), axis=1), expand_ref[...])
    dt_x, decay_in, decay_out = expanded[0:Q], expanded[Q:2 * Q], expanded[2 * Q:3 * Q]

    xdt = xs * dt_x
    xdt_bf = xdt.astype(BF16)
    xdec_bf = (xdt * decay_out).astype(BF16)
    row = lax.broadcasted_iota(jnp.int32, (Q, Q), 0)
    col = lax.broadcasted_iota(jnp.int32, (Q, Q), 1)
    mask = (row >= col) if not reverse else (row <= col)
    gw = HEADS_PER_GROUP * HEAD_DIM
    for g in range(GROUPS):
        cg = c_bf[:, g * STATE:(g + 1) * STATE]
        bg = b_bf[:, g * STATE:(g + 1) * STATE]
        scores = lax.dot_general(cg, bg, (((1,), (1,)), ((), ())), preferred_element_type=F32)
        state = st_ref[0, g]
        y_off = _dot(cg, state.astype(BF16))
        heads = []
        for hh in range(HEADS_PER_GROUP):
            h = g * HEADS_PER_GROUP + hh
            ln = direction * HEADS + h
            seg = jnp.where(mask, jnp.exp(acum[:, ln:ln + 1] - acum_t[ln:ln + 1, :]), 0.0)
            heads.append(_dot((scores * seg).astype(BF16), xdt_bf[:, h * HEAD_DIM:(h + 1) * HEAD_DIM]))
        y_diag = jnp.concatenate(heads, axis=1)
        sl = slice(g * gw, (g + 1) * gw)
        y_ref[0, rows, sl] = y_diag + y_off * decay_in[:, sl] + dskip_ref[:, sl] * xs[:, sl]
        upd = lax.dot_general(bg, xdec_bf[:, sl], (((0,), (0,)), ((), ())), preferred_element_type=F32)
        st_ref[0, g] = state * decay_in[last:last + 1, sl] + upd


def _ssd_scan(xbc, dt, h0, lp, direction):
    B, L, _ = xbc.shape
    Q = SSD_CHUNK
    block = min(SSD_BLOCK, L)
    n_blocks = L // block
    reverse = direction == 1
    chunk = (lambda i: n_blocks - 1 - i) if reverse else (lambda i: i)
    tri = np.tril(np.ones((Q, Q), np.float32))
    if reverse:
        tri = tri.T
    tri3 = jnp.asarray(np.concatenate([tri, tri, tri], axis=1), BF16)
    expand = np.zeros((LANES, D_SSM), np.float32)
    for h in range(HEADS):
        expand[direction * HEADS + h, h * HEAD_DIM:(h + 1) * HEAD_DIM] = 1.0
    expand2 = jnp.asarray(np.concatenate([expand, expand], axis=0), BF16)
    gw = HEADS_PER_GROUP * HEAD_DIM
    const = lambda shape: pl.BlockSpec(shape, lambda b, i: (0,) * len(shape))
    state_spec = pl.BlockSpec((1, GROUPS, STATE, gw), lambda b, i: (b, 0, 0, 0))
    return pl.pallas_call(
        functools.partial(_ssd_kernel, reverse=reverse, direction=direction),
        out_shape=[jax.ShapeDtypeStruct((B, L, D_SSM), F32), jax.ShapeDtypeStruct((B, GROUPS, STATE, gw), F32)],
        grid=(B, n_blocks),
        in_specs=[
            pl.BlockSpec((1, block, D_XBC), lambda b, i: (b, chunk(i), 0)),
            pl.BlockSpec((1, block, DT_PAD), lambda b, i: (b, chunk(i), 0)),
            state_spec,
            const((SSM_CONV_K, D_XBC)), const((1, D_XBC)), const((1, DT_PAD)), const((1, DT_PAD)),
            const((1, D_SSM)), const((Q, 3 * Q)), const((2 * LANES, D_SSM)),
        ],
        out_specs=[pl.BlockSpec((1, block, D_SSM), lambda b, i: (b, chunk(i), 0)), state_spec],
        scratch_shapes=[pltpu.VMEM((Q + 2 * SUBLANES, D_XBC), F32)],
        compiler_params=_params(("arbitrary", "arbitrary")),
        name="ssd_scan_bwd" if reverse else "ssd_scan_fwd",
    )(xbc, dt, h0, lp["ssm_conv_w"][direction], lp["ssm_conv_b"][direction], lp["dt_bias"], lp["a_log"],
      lp["d_skip"][direction], tri3, expand2)


def _layer_norm(v, gain, bias):
    mu = jnp.mean(v, axis=-1, keepdims=True)
    d = v - mu
    var = jnp.mean(d * d, axis=-1, keepdims=True)
    return d * lax.rsqrt(var + LN_EPS) * gain + bias


MIX_ROWS = 64


def _mix_kernel(a_ref, ap_ref, an_ref, up_ref, upp_ref, upn_ref, yf_ref, yb_ref, z_ref, x_ref, gate_ref,
                dww_ref, dwb_ref, clg_ref, clb_ref, pww_ref, pwb_ref, ng_ref, plw_ref, pls_ref, wo_ref,
                l1g_ref, l1b_ref, o_ref, ext_ref, pext_ref, conv_ref, pool_ref, *, tile, seq_len):
    i = pl.program_id(1)
    n_tiles = seq_len // tile
    has_prev = jnp.where(i > 0, 1.0, 0.0)
    has_next = jnp.where(i < n_tiles - 1, 1.0, 0.0)

    def glu(v):
        return v[:, :D_CONV] * jax.nn.sigmoid(v[:, D_CONV:])

    ext_ref[0, 0:HALO, :] = glu(ap_ref[0]) * has_prev
    ext_ref[0, HALO:HALO + tile, :] = glu(a_ref[0])
    ext_ref[0, HALO + tile:, :] = glu(an_ref[0]) * has_next
    pext_ref[0, 0:HALO, :] = upp_ref[0] * has_prev
    pext_ref[0, HALO:HALO + tile, :] = up_ref[0]
    pext_ref[0, HALO + tile:, :] = upn_ref[0] * has_next
    n_shift = tile + 2 * HALO - SUBLANES
    for s in range(1, SUBLANES):
        ext_ref[s, 0:n_shift, :] = ext_ref[0, s:s + n_shift, :]
        pext_ref[s, 0:n_shift, :] = pext_ref[0, s:s + n_shift, :]

    def tap(ref, o):
        s = o % SUBLANES
        return ref[s, o - s:o - s + MIX_ROWS, :]

    lane = lax.broadcasted_iota(jnp.int32, (MIX_ROWS, D_POOL), 1)
    group = lane // POOL_GROUP
    row = lax.broadcasted_iota(jnp.int32, (MIX_ROWS, D_POOL), 0)
    half_c = CONV_K // 2
    for r in range(tile // MIX_ROWS):
        base = r * MIX_ROWS
        acc = dwb_ref[...] + dww_ref[0:1, :] * tap(ext_ref, base + HALO - half_c)
        for k in range(1, CONV_K):
            acc = acc + dww_ref[k:k + 1, :] * tap(ext_ref, base + HALO - half_c + k)
        conv_ref[base:base + MIX_ROWS, :] = acc
        c = base + HALO
        sums = []
        s = tap(pext_ref, c - 1) + tap(pext_ref, c)
        sums.append(s)
        lo, hi = 1, 0
        for w in POOL_WINDOWS[1:]:
            for j in range(w // 2, lo, -1):
                s = s + tap(pext_ref, c - j)
            for j in range(hi + 1, w - w // 2):
                s = s + tap(pext_ref, c + j)
            lo, hi = w // 2, w - w // 2 - 1
            sums.append(s)
        t = row + (i * tile + base)
        mean = jnp.zeros((MIX_ROWS, D_POOL), F32)
        for gi, w in enumerate(POOL_WINDOWS):
            cnt = jnp.minimum(t + (w - w // 2), seq_len) - jnp.maximum(t - w // 2, 0)
            mean = jnp.where(group == gi, sums[gi] / cnt.astype(F32), mean)
        pool_ref[base:base + MIX_ROWS, :] = mean - tap(pext_ref, c)

    ya = _silu(_layer_norm(conv_ref[...], clg_ref[...], clb_ref[...]))
    ya = _dot(ya.astype(BF16), pww_ref[...]) + pwb_ref[...]
    yc = _dot(pool_ref[...].astype(BF16), plw_ref[...]) * pls_ref[...]
    v = (yf_ref[0] + yb_ref[0]) * _silu(z_ref[0])
    yb = v * lax.rsqrt(jnp.mean(v * v, axis=-1, keepdims=True) + RMS_EPS) * ng_ref[...]
    mix = (_dot(ya.astype(BF16), wo_ref[0:D_CONV, :])
           + _dot(yb.astype(BF16), wo_ref[D_CONV:D_CONV + D_SSM, :])
           + _dot(yc.astype(BF16), wo_ref[D_CONV + D_SSM:, :]))
    o_ref[0] = _layer_norm(ALPHA * x_ref[0] + gate_ref[0] * mix, l1g_ref[...], l1b_ref[...])


def _mixers(a, up, y_f, y_b, z, x, gate, lp, tile):
    B, L, _ = x.shape
    hb = tile // HALO
    n_halo = L // HALO
    tok = lambda w: pl.BlockSpec((1, tile, w), lambda b, i: (b, i, 0))
    prev = lambda w: pl.BlockSpec((1, HALO, w), lambda b, i: (b, jnp.maximum(i * hb - 1, 0), 0))
    nxt = lambda w: pl.BlockSpec((1, HALO, w), lambda b, i: (b, jnp.minimum((i + 1) * hb, n_halo - 1), 0))
    const = lambda shape: pl.BlockSpec(shape, lambda b, i: (0,) * len(shape))
    return pl.pallas_call(
        functools.partial(_mix_kernel, tile=tile, seq_len=L),
        out_shape=jax.ShapeDtypeStruct((B, L, D_MODEL), F32),
        grid=(B, L // tile),
        in_specs=[
            tok(2 * D_CONV), prev(2 * D_CONV), nxt(2 * D_CONV),
            tok(D_POOL), prev(D_POOL), nxt(D_POOL),
            tok(D_SSM), tok(D_SSM), tok(D_SSM), tok(D_MODEL),
            pl.BlockSpec((1, 1, D_MODEL), lambda b, i: (b, 0, 0)),
            const((CONV_K, D_CONV)), const((1, D_CONV)), const((1, D_CONV)), const((1, D_CONV)),
            const((D_CONV, D_CONV)), const((1, D_CONV)), const((1, D_SSM)),
            const((D_POOL, D_POOL)), const((1, D_POOL)), const((D_MODEL, D_MODEL)),
            const((1, D_MODEL)), const((1, D_MODEL)),
        ],
        out_specs=tok(D_MODEL),
        scratch_shapes=[pltpu.VMEM((SUBLANES, tile + 2 * HALO, D_CONV), F32),
                        pltpu.VMEM((SUBLANES, tile + 2 * HALO, D_POOL), F32),
                        pltpu.VMEM((tile, D_CONV), F32), pltpu.VMEM((tile, D_POOL), F32)],
        compiler_params=_params(("arbitrary", "arbitrary")),
        name="mixers_outproj_ln1",
    )(a, a, a, up, up, up, y_f, y_b, z, x, gate,
      lp["conv_dw_w"], lp["conv_dw_b"], lp["conv_ln_g"], lp["conv_ln_b"], lp["conv_pw_w"], lp["conv_pw_b"],
      lp["ssm_norm_g"], lp["pool_w"], lp["pool_scale"], lp["w_out"], lp["ln1_g"], lp["ln1_b"])


def _ffn_kernel(*refs, rows, width, has_halo):
    n_x = 3 if has_halo else 1
    x_ref = refs[0]
    (sh_ref, sc_ref, gate_ref, wv_ref, wg_ref, wd_ref, dww_ref, dwb_ref, l2g_ref, l2b_ref,
     o_ref, h_ref, g0a_ref, g0b_ref, vala_ref, valb_ref, gm_ref, gp_ref, act_ref) = refs[n_x:]
    tile = rows * width
    ext_rows = rows + 2 if has_halo else rows
    first = width if has_halo else 0
    i = pl.program_id(1)
    n_tiles = pl.num_programs(1)

    def modulate(v):
        return v * (1.0 + sc_ref[0]) + sh_ref[0]

    h_ref[first:first + tile, :] = modulate(x_ref[0]).astype(BF16)
    if has_halo:
        xp_ref, xn_ref = refs[1], refs[2]
        h_ref[0:width, :] = (modulate(xp_ref[0]) * jnp.where(i > 0, 1.0, 0.0)).astype(BF16)
        h_ref[first + tile:, :] = (modulate(xn_ref[0]) * jnp.where(i < n_tiles - 1, 1.0, 0.0)).astype(BF16)
    tok = lax.broadcasted_iota(jnp.int32, (width, FF_CHUNK), 0)
    not_first = tok != 0
    not_last = tok != width - 1

    def up_project(j, g0_ref, val_ref):
        g0_ref[...] = _dot(h_ref[...], wg_ref[j])
        val_ref[...] = _dot(h_ref[first:first + tile, :], wv_ref[j])

    def conv_gate(j, g0_ref, val_ref):
        for e in range(ext_rows):
            blk = g0_ref[e * width:(e + 1) * width, :]
            gm_ref[e * width:(e + 1) * width, :] = jnp.where(not_first, pltpu.roll(blk, 1, 0), 0.0)
            gp_ref[e * width:(e + 1) * width, :] = jnp.where(not_last, pltpu.roll(blk, width - 1, 0), 0.0)
        tap = lambda k: dww_ref[j, k:k + 1, :]
        for r in range(rows):
            conv = dwb_ref[j]
            for dr in range(FFN_K):
                e = r + dr - 1 + (1 if has_halo else 0)
                if e < 0 or e >= ext_rows:
                    continue
                sl = slice(e * width, (e + 1) * width)
                conv = (conv + tap(3 * dr) * gm_ref[sl, :] + tap(3 * dr + 1) * g0_ref[sl, :]
                        + tap(3 * dr + 2) * gp_ref[sl, :])
            out = slice(r * width, (r + 1) * width)
            act_ref[out, j * FF_CHUNK:(j + 1) * FF_CHUNK] = (val_ref[out, :] * jax.nn.gelu(conv)).astype(BF16)

    slots = ((g0a_ref, vala_ref), (g0b_ref, valb_ref))
    up_project(0, *slots[0])
    for j in range(N_FF_CHUNKS):
        if j + 1 < N_FF_CHUNKS:
            up_project(j + 1, *slots[(j + 1) % 2])
        conv_gate(j, *slots[j % 2])
    f = _dot(act_ref[...], wd_ref[...])
    o_ref[0] = _layer_norm(ALPHA * x_ref[0] + gate_ref[0] * f, l2g_ref[...], l2b_ref[...])


def _conv_ffn(x, shift, scale, gate, lp, rows, width):
    B, L, _ = x.shape
    tile = rows * width
    n_tiles = L // tile
    has_halo = n_tiles * rows > 1
    n_rows = L // width
    ext = (rows + 2) * width if has_halo else tile
    tok = pl.BlockSpec((1, tile, D_MODEL), lambda b, i: (b, i, 0))
    vec = pl.BlockSpec((1, 1, D_MODEL), lambda b, i: (b, 0, 0))
    const = lambda shape: pl.BlockSpec(shape, lambda b, i: (0,) * len(shape), pipeline_mode=pl.Buffered(1))
    x_specs, x_args = [tok], [x]
    if has_halo:
        x_specs += [pl.BlockSpec((1, width, D_MODEL), lambda b, i: (b, jnp.maximum(i * rows - 1, 0), 0)),
                    pl.BlockSpec((1, width, D_MODEL), lambda b, i: (b, jnp.minimum((i + 1) * rows, n_rows - 1), 0))]
        x_args += [x, x]
    return pl.pallas_call(
        functools.partial(_ffn_kernel, rows=rows, width=width, has_halo=has_halo),
        out_shape=jax.ShapeDtypeStruct((B, L, D_MODEL), F32),
        grid=(B, n_tiles),
        in_specs=x_specs + [
            vec, vec, vec,
            const((N_FF_CHUNKS, D_MODEL, FF_CHUNK)), const((N_FF_CHUNKS, D_MODEL, FF_CHUNK)),
            const((D_FF, D_MODEL)),
            const((N_FF_CHUNKS, FFN_K * FFN_K, FF_CHUNK)), const((N_FF_CHUNKS, 1, FF_CHUNK)),
            const((1, D_MODEL)), const((1, D_MODEL)),
        ],
        out_specs=tok,
        scratch_shapes=[pltpu.VMEM((ext, D_MODEL), BF16),
                        pltpu.VMEM((ext, FF_CHUNK), F32), pltpu.VMEM((ext, FF_CHUNK), F32),
                        pltpu.VMEM((tile, FF_CHUNK), F32), pltpu.VMEM((tile, FF_CHUNK), F32),
                        pltpu.VMEM((ext, FF_CHUNK), F32), pltpu.VMEM((ext, FF_CHUNK), F32),
                        pltpu.VMEM((tile, D_FF), BF16)],
        compiler_params=_params(("arbitrary", "arbitrary")),
        name="conv_ffn_ln2",
    )(*x_args, shift, scale, gate, lp["w_up_val"], lp["w_up_gate"], lp["w_down"], lp["ffn_dw_w"], lp["ffn_dw_b"],
      lp["ln2_g"], lp["ln2_b"])


def _layer_params(l, w_in, conv_dw_w, conv_dw_b, conv_ln_g, conv_ln_b, conv_pw_w, conv_pw_b, ssm_conv_w, ssm_conv_b,
                  ssm_dt_bias, ssm_A_log, ssm_D, ssm_norm_g, pool_w, pool_scale, w_out, ln1_g, ln1_b, w_up, ffn_dw_w,
                  ffn_dw_b, w_down, ln2_g, ln2_b):
    row = lambda v: v.reshape(1, -1)
    wi = w_in[l]
    s0, s1, s2, s3, s4 = 2 * D_CONV, 2 * D_CONV + D_SSM, 2 * D_CONV + D_SSM + D_XBC, 2 * D_CONV + D_SSM + D_XBC + 2 * HEADS, wi.shape[1]
    w_in_r = jnp.concatenate([wi[:, :s2], wi[:, s3:s4], wi[:, s2:s3], jnp.zeros((D_MODEL, DT_PAD - 2 * HEADS), F32)],
                             axis=1).astype(BF16)
    pad_lanes = lambda v: jnp.pad(v.reshape(1, -1), ((0, 0), (0, DT_PAD - 2 * HEADS)))
    pool_bd = jnp.zeros((D_POOL, D_POOL), F32)
    for g in range(len(POOL_WINDOWS)):
        sl = slice(g * POOL_GROUP, (g + 1) * POOL_GROUP)
        pool_bd = pool_bd.at[sl, sl].set(pool_w[l, g])
    chunks = lambda v: v.reshape(v.shape[0], N_FF_CHUNKS, FF_CHUNK).transpose(1, 0, 2)
    return {
        "w_in": w_in_r,
        "conv_dw_w": conv_dw_w[l], "conv_dw_b": row(conv_dw_b[l]), "conv_ln_g": row(conv_ln_g[l]),
        "conv_ln_b": row(conv_ln_b[l]), "conv_pw_w": conv_pw_w[l].astype(BF16), "conv_pw_b": row(conv_pw_b[l]),
        "ssm_conv_w": ssm_conv_w[l], "ssm_conv_b": ssm_conv_b[l].reshape(2, 1, D_XBC),
        "dt_bias": pad_lanes(ssm_dt_bias[l]), "a_log": pad_lanes(ssm_A_log[l]),
        "d_skip": jnp.repeat(ssm_D[l], HEAD_DIM, axis=1).reshape(2, 1, D_SSM),
        "ssm_norm_g": row(ssm_norm_g[l]), "pool_w": pool_bd.astype(BF16), "pool_scale": row(pool_scale[l]),
        "w_out": w_out[l].astype(BF16), "ln1_g": row(ln1_g[l]), "ln1_b": row(ln1_b[l]),
        "w_up_val": chunks(w_up[l, :, :D_FF]).astype(BF16), "w_up_gate": chunks(w_up[l, :, D_FF:]).astype(BF16),
        "w_down": w_down[l].astype(BF16),
        "ffn_dw_w": chunks(ffn_dw_w[l].reshape(FFN_K * FFN_K, D_FF)), "ffn_dw_b": chunks(ffn_dw_b[l].reshape(1, D_FF)),
        "ln2_g": row(ln2_g[l]), "ln2_b": row(ln2_b[l]),
    }


def _ssd_branch(h, shift, scale, lp, h0_f, h0_b, tile):
    a, z, xbc, up, dt = _in_projection(h, shift, scale, lp["w_in"], tile)
    y_f, st_f = _ssd_scan(xbc, dt, h0_f, lp, 0)
    y_b, st_b = _ssd_scan(xbc, dt, h0_b, lp, 1)
    return (a, up, y_f, y_b, z), st_f, st_b


def kernel(x, c, ctx, c_ctx, w_mod, b_mod, w_in, conv_dw_w, conv_dw_b, conv_ln_g, conv_ln_b, conv_pw_w, conv_pw_b,
           ssm_conv_w, ssm_conv_b, ssm_dt_bias, ssm_A_log, ssm_D, ssm_norm_g, pool_w, pool_scale, w_out, ln1_g, ln1_b,
           w_up, ffn_dw_w, ffn_dw_b, w_down, ln2_g, ln2_b):
    B, L, _ = x.shape
    ctx_len = ctx.shape[1]
    assert L % (8 * GRID_W) == 0 and ctx_len % SSD_CHUNK == 0 and B + 1 <= MOD_ROWS
    cvec = jnp.concatenate([c, c_ctx[None, :], jnp.zeros((MOD_ROWS - B - 1, D_MODEL), F32)], axis=0)
    mods = _modulation(cvec, w_mod, b_mod.reshape(DEPTH, 1, N_MOD * D_MODEL))
    zero_state = jnp.zeros((B, GROUPS, STATE, HEADS_PER_GROUP * HEAD_DIM), F32)
    lat_tile = 8 * GRID_W
    for l in range(DEPTH):
        last = l == DEPTH - 1
        lp = _layer_params(l, w_in, conv_dw_w, conv_dw_b, conv_ln_g, conv_ln_b, conv_pw_w, conv_pw_b, ssm_conv_w,
                           ssm_conv_b, ssm_dt_bias, ssm_A_log, ssm_D, ssm_norm_g, pool_w, pool_scale, w_out, ln1_g,
                           ln1_b, w_up, ffn_dw_w, ffn_dw_b, w_down, ln2_g, ln2_b)
        mx = [mods[l, :B, k * D_MODEL:(k + 1) * D_MODEL].reshape(B, 1, D_MODEL) for k in range(N_MOD)]
        mc = [jnp.broadcast_to(mods[l, B, k * D_MODEL:(k + 1) * D_MODEL].reshape(1, 1, D_MODEL), (B, 1, D_MODEL))
              for k in range(N_MOD)]
        (a, up, y_f, y_b, z), st_f, st_b = _ssd_branch(ctx, mc[0], mc[1], lp, zero_state, zero_state, ctx_len)
        if not last:
            ctx1 = _mixers(a, up, y_f, y_b, z, ctx, mc[2], lp, ctx_len)
            ctx = _conv_ffn(ctx1, mc[3], mc[4], mc[5], lp, 1, ctx_len)
        (a, up, y_f, y_b, z), _, _ = _ssd_branch(x, mx[0], mx[1], lp, st_f, st_b, lat_tile)
        x1 = _mixers(a, up, y_f, y_b, z, x, mx[2], lp, lat_tile)
        x = _conv_ffn(x1, mx[3], mx[4], mx[5], lp, 8, GRID_W)
    return x
```

```python
import functools

import numpy as np
import jax
import jax.numpy as jnp
from jax import lax
from jax.experimental import pallas as pl
from jax.experimental.pallas import tpu as pltpu

F32 = jnp.float32
BF16 = jnp.bfloat16

D_MODEL = 1024
DEPTH = 4
GRID_W = 64
D_CONV = 256
CONV_K = 31
D_SSM = 512
HEAD_DIM = 64
HEADS = 8
GROUPS = 2
HEADS_PER_GROUP = HEADS // GROUPS
STATE = 128
SSM_CONV_K = 4
D_XBC = D_SSM + 2 * GROUPS * STATE
D_POOL = 256
POOL_WINDOWS = (2, 4, 8, 16)
POOL_GROUP = D_POOL // len(POOL_WINDOWS)
D_FF = 2816
FFN_K = 3
N_MOD = 6
ALPHA = (2.0 * DEPTH) ** 0.25
LN_EPS = 1e-5
RMS_EPS = 1e-5

LANES = 128
SUBLANES = 8
HALO = 16
DT_PAD = LANES
IN_COLS_PAD = 2 * D_CONV + D_SSM + D_XBC + D_POOL + DT_PAD
SSD_CHUNK = 128
SSD_BLOCK = 512
FF_CHUNK = 256
N_FF_CHUNKS = D_FF // FF_CHUNK
MOD_ROWS = 16
MOD_TILE = 1024
VMEM_LIMIT = 52 * 1024 * 1024


def _silu(v):
    return v * jax.nn.sigmoid(v)


def _dot(a, b):
    return jnp.dot(a, b, preferred_element_type=F32)


def _params(sem):
    return pltpu.CompilerParams(dimension_semantics=sem, vmem_limit_bytes=VMEM_LIMIT)


def _mod_kernel(c_ref, w_ref, b_ref, o_ref):
    s = _silu(c_ref[...])
    o_ref[0] = _dot(s.astype(BF16), w_ref[0].astype(BF16)) + b_ref[0]


def _modulation(cvec, w_mod, b_mod):
    n_cols = N_MOD * D_MODEL
    return pl.pallas_call(
        _mod_kernel,
        out_shape=jax.ShapeDtypeStruct((DEPTH, MOD_ROWS, n_cols), F32),
        grid=(DEPTH, n_cols // MOD_TILE),
        in_specs=[
            pl.BlockSpec((MOD_ROWS, D_MODEL), lambda l, j: (0, 0)),
            pl.BlockSpec((1, D_MODEL, MOD_TILE), lambda l, j: (l, 0, j)),
            pl.BlockSpec((1, 1, MOD_TILE), lambda l, j: (l, 0, j)),
        ],
        out_specs=pl.BlockSpec((1, MOD_ROWS, MOD_TILE), lambda l, j: (l, 0, j)),
        compiler_params=_params(("arbitrary", "arbitrary")),
        name="adaln_modulation",
    )(cvec, w_mod, b_mod)


_A0, _Z0, _X0, _P0, _T0 = 0, 2 * D_CONV, 2 * D_CONV + D_SSM, 2 * D_CONV + D_SSM + D_XBC, 2 * D_CONV + D_SSM + D_XBC + D_POOL


def _inproj_kernel(x_ref, sh_ref, sc_ref, w_ref, a_ref, z_ref, xbc_ref, up_ref, dt_ref):
    h = (x_ref[0] * (1.0 + sc_ref[0]) + sh_ref[0]).astype(BF16)
    a_ref[0] = _dot(h, w_ref[:, _A0:_Z0])
    z_ref[0] = _dot(h, w_ref[:, _Z0:_X0])
    xbc_ref[0] = _dot(h, w_ref[:, _X0:_P0])
    up_ref[0] = _dot(h, w_ref[:, _P0:_T0])
    dt_ref[0] = _dot(h, w_ref[:, _T0:IN_COLS_PAD])


def _in_projection(x, shift, scale, w_in, tile):
    B, L, _ = x.shape
    widths = (2 * D_CONV, D_SSM, D_XBC, D_POOL, DT_PAD)
    tok = lambda w: pl.BlockSpec((1, tile, w), lambda b, i: (b, i, 0))
    vec = pl.BlockSpec((1, 1, D_MODEL), lambda b, i: (b, 0, 0))
    return pl.pallas_call(
        _inproj_kernel,
        out_shape=[jax.ShapeDtypeStruct((B, L, w), F32) for w in widths],
        grid=(B, L // tile),
        in_specs=[tok(D_MODEL), vec, vec, pl.BlockSpec((D_MODEL, IN_COLS_PAD), lambda b, i: (0, 0))],
        out_specs=[tok(w) for w in widths],
        compiler_params=_params(("arbitrary", "arbitrary")),
        name="in_projection",
    )(x, shift, scale, w_in)


def _split_bf16(v, parts):
    out = []
    for _ in range(parts - 1):
        p = v.astype(BF16)
        out.append(p)
        v = v - p.astype(F32)
    out.append(v.astype(BF16))
    return out


def _ssd_kernel(xf_ref, dtf_ref, xb_ref, dtb_ref, h0f_ref, h0b_ref, cw_ref, cb_ref, dtbias_ref, alog_ref, dskip_ref,
                tri_ref, expand_ref, yf_ref, yb_ref, stf_ref, stb_ref, extf_ref, extb_ref):
    @pl.when(pl.program_id(1) == 0)
    def _():
        stf_ref[...] = h0f_ref[...]
        stb_ref[...] = h0b_ref[...]
        extf_ref[...] = jnp.zeros_like(extf_ref)
        extb_ref[...] = jnp.zeros_like(extb_ref)

    n_sub = xf_ref.shape[1] // SSD_CHUNK
    for sub in range(n_sub):
        _ssd_chunk(xf_ref, dtf_ref, cw_ref, cb_ref, dtbias_ref, alog_ref, dskip_ref, tri_ref, expand_ref, yf_ref,
                   stf_ref, extf_ref, sub * SSD_CHUNK, 0)
        _ssd_chunk(xb_ref, dtb_ref, cw_ref, cb_ref, dtbias_ref, alog_ref, dskip_ref, tri_ref, expand_ref, yb_ref,
                   stb_ref, extb_ref, (n_sub - 1 - sub) * SSD_CHUNK, 1)


def _ssd_chunk(xbc_ref, dt_ref, cw_ref, cb_ref, dtb_ref, alog_ref, dskip_ref, tri_ref, expand_ref, y_ref, st_ref,
               ext_ref, r0, direction):
    Q = SSD_CHUNK
    reverse = direction == 1
    rows = slice(r0, r0 + Q)
    xbc = xbc_ref[0, rows, :]
    conv = cb_ref[direction]
    if not reverse:
        ext_ref[SUBLANES:SUBLANES + Q, :] = xbc
        for k in range(SSM_CONV_K):
            off = SUBLANES - (SSM_CONV_K - 1) + k
            conv = conv + cw_ref[direction, k:k + 1, :] * ext_ref[off:off + Q, :]
        ext_ref[0:SUBLANES, :] = xbc[Q - SUBLANES:Q, :]
    else:
        ext_ref[0:Q, :] = xbc
        for k in range(SSM_CONV_K):
            off = SSM_CONV_K - 1 - k
            conv = conv + cw_ref[direction, k:k + 1, :] * ext_ref[off:off + Q, :]
        ext_ref[Q:Q + SUBLANES, :] = xbc[0:SUBLANES, :]
    u = _silu(conv)
    xs = u[:, :D_SSM]
    b_bf = u[:, D_SSM:D_SSM + GROUPS * STATE].astype(BF16)
    c_bf = u[:, D_SSM + GROUPS * STATE:].astype(BF16)

    lane = lax.broadcasted_iota(jnp.int32, (1, LANES), 1)
    own = (lane >= direction * HEADS) & (lane < (direction + 1) * HEADS)
    a_head = jnp.where(own, -jnp.exp(alog_ref[...]), 0.0)
    dt = jax.nn.softplus(dt_ref[0, rows, :] + dtb_ref[...])
    a = dt * a_head
    acum = _dot(tri_ref[direction], jnp.concatenate(_split_bf16(a, 3), axis=0))
    acum_t = acum.T
    last = Q - 1 if not reverse else 0
    total = acum[last:last + 1, :]
    stacked = jnp.concatenate([dt, jnp.exp(acum), jnp.exp(total - acum)], axis=0)
    expanded = _dot(jnp.concatenate(_split_bf16(stacked, 2), axis=1), expand_ref[direction])
    dt_x, decay_in, decay_out = expanded[0:Q], expanded[Q:2 * Q], expanded[2 * Q:3 * Q]

    xdt = xs * dt_x
    xdt_bf = xdt.astype(BF16)
    xdec_bf = (xdt * decay_out).astype(BF16)
    row = lax.broadcasted_iota(jnp.int32, (Q, Q), 0)
    col = lax.broadcasted_iota(jnp.int32, (Q, Q), 1)
    mask = (row >= col) if not reverse else (row <= col)
    gw = HEADS_PER_GROUP * HEAD_DIM
    for g in range(GROUPS):
        cg = c_bf[:, g * STATE:(g + 1) * STATE]
        bg = b_bf[:, g * STATE:(g + 1) * STATE]
        scores = lax.dot_general(cg, bg, (((1,), (1,)), ((), ())), preferred_element_type=F32)
        state = st_ref[0, g]
        y_off = _dot(cg, state.astype(BF16))
        heads = []
        for hh in range(HEADS_PER_GROUP):
            h = g * HEADS_PER_GROUP + hh
            ln = direction * HEADS + h
            seg = jnp.where(mask, jnp.exp(acum[:, ln:ln + 1] - acum_t[ln:ln + 1, :]), 0.0)
            heads.append(_dot((scores * seg).astype(BF16), xdt_bf[:, h * HEAD_DIM:(h + 1) * HEAD_DIM]))
        y_diag = jnp.concatenate(heads, axis=1)
        sl = slice(g * gw, (g + 1) * gw)
        y_ref[0, rows, sl] = y_diag + y_off * decay_in[:, sl] + dskip_ref[direction, :, sl] * xs[:, sl]
        upd = lax.dot_general(bg, xdec_bf[:, sl], (((0,), (0,)), ((), ())), preferred_element_type=F32)
        st_ref[0, g] = state * decay_in[last:last + 1, sl] + upd


def _ssd_scan(xbc, dt, h0_f, h0_b, lp):
    B, L, _ = xbc.shape
    Q = SSD_CHUNK
    block = min(SSD_BLOCK, L)
    n_blocks = L // block
    tri = np.tril(np.ones((Q, Q), np.float32))
    tri3 = jnp.asarray(np.stack([np.concatenate([t, t, t], axis=1) for t in (tri, tri.T)]), BF16)
    expand = np.zeros((2, LANES, D_SSM), np.float32)
    for d in range(2):
        for h in range(HEADS):
            expand[d, d * HEADS + h, h * HEAD_DIM:(h + 1) * HEAD_DIM] = 1.0
    expand2 = jnp.asarray(np.concatenate([expand, expand], axis=1), BF16)
    gw = HEADS_PER_GROUP * HEAD_DIM
    const = lambda shape: pl.BlockSpec(shape, lambda b, i: (0,) * len(shape))
    fwd = lambda w: pl.BlockSpec((1, block, w), lambda b, i: (b, i, 0))
    bwd = lambda w: pl.BlockSpec((1, block, w), lambda b, i: (b, n_blocks - 1 - i, 0))
    state_spec = pl.BlockSpec((1, GROUPS, STATE, gw), lambda b, i: (b, 0, 0, 0))
    y_shape = jax.ShapeDtypeStruct((B, L, D_SSM), F32)
    st_shape = jax.ShapeDtypeStruct((B, GROUPS, STATE, gw), F32)
    return pl.pallas_call(
        _ssd_kernel,
        out_shape=[y_shape, y_shape, st_shape, st_shape],
        grid=(B, n_blocks),
        in_specs=[
            fwd(D_XBC), fwd(DT_PAD), bwd(D_XBC), bwd(DT_PAD), state_spec, state_spec,
            const((2, SSM_CONV_K, D_XBC)), const((2, 1, D_XBC)), const((1, DT_PAD)), const((1, DT_PAD)),
            const((2, 1, D_SSM)), const((2, Q, 3 * Q)), const((2, 2 * LANES, D_SSM)),
        ],
        out_specs=[fwd(D_SSM), bwd(D_SSM), state_spec, state_spec],
        scratch_shapes=[pltpu.VMEM((Q + 2 * SUBLANES, D_XBC), F32), pltpu.VMEM((Q + 2 * SUBLANES, D_XBC), F32)],
        compiler_params=_params(("arbitrary", "arbitrary")),
        name="ssd_scan",
    )(xbc, dt, xbc, dt, h0_f, h0_b, lp["ssm_conv_w"], lp["ssm_conv_b"], lp["dt_bias"], lp["a_log"], lp["d_skip"],
      tri3, expand2)


def _layer_norm(v, gain, bias):
    mu = jnp.mean(v, axis=-1, keepdims=True)
    d = v - mu
    var = jnp.mean(d * d, axis=-1, keepdims=True)
    return d * lax.rsqrt(var + LN_EPS) * gain + bias


MIX_ROWS = 64


def _mix_kernel(a_ref, ap_ref, an_ref, up_ref, upp_ref, upn_ref, yf_ref, yb_ref, z_ref, x_ref, gate_ref,
                dww_ref, dwb_ref, clg_ref, clb_ref, pww_ref, pwb_ref, ng_ref, plw_ref, pls_ref, wo_ref,
                l1g_ref, l1b_ref, o_ref, ext_ref, pext_ref, conv_ref, pool_ref, *, tile, seq_len):
    i = pl.program_id(1)
    n_tiles = seq_len // tile
    has_prev = jnp.where(i > 0, 1.0, 0.0)
    has_next = jnp.where(i < n_tiles - 1, 1.0, 0.0)

    def glu(v):
        return v[:, :D_CONV] * jax.nn.sigmoid(v[:, D_CONV:])

    ext_ref[0, 0:HALO, :] = glu(ap_ref[0]) * has_prev
    ext_ref[0, HALO:HALO + tile, :] = glu(a_ref[0])
    ext_ref[0, HALO + tile:, :] = glu(an_ref[0]) * has_next
    pext_ref[0, 0:HALO, :] = upp_ref[0] * has_prev
    pext_ref[0, HALO:HALO + tile, :] = up_ref[0]
    pext_ref[0, HALO + tile:, :] = upn_ref[0] * has_next
    n_shift = tile + 2 * HALO - SUBLANES
    for s in range(1, SUBLANES):
        ext_ref[s, 0:n_shift, :] = ext_ref[0, s:s + n_shift, :]
        pext_ref[s, 0:n_shift, :] = pext_ref[0, s:s + n_shift, :]

    def tap(ref, o):
        s = o % SUBLANES
        return ref[s, o - s:o - s + MIX_ROWS, :]

    lane = lax.broadcasted_iota(jnp.int32, (MIX_ROWS, D_POOL), 1)
    group = lane // POOL_GROUP
    row = lax.broadcasted_iota(jnp.int32, (MIX_ROWS, D_POOL), 0)
    back = jnp.zeros((MIX_ROWS, D_POOL), jnp.int32)
    ahead = jnp.zeros((MIX_ROWS, D_POOL), jnp.int32)
    for gi, w in enumerate(POOL_WINDOWS):
        back = jnp.where(group == gi, w // 2, back)
        ahead = jnp.where(group == gi, w - w // 2, ahead)
    half_c = CONV_K // 2
    for r in range(tile // MIX_ROWS):
        base = r * MIX_ROWS
        acc = dwb_ref[...] + dww_ref[0:1, :] * tap(ext_ref, base + HALO - half_c)
        for k in range(1, CONV_K):
            acc = acc + dww_ref[k:k + 1, :] * tap(ext_ref, base + HALO - half_c + k)
        conv_ref[base:base + MIX_ROWS, :] = acc
        c = base + HALO
        sums = []
        s = tap(pext_ref, c - 1) + tap(pext_ref, c)
        sums.append(s)
        lo, hi = 1, 0
        for w in POOL_WINDOWS[1:]:
            for j in range(w // 2, lo, -1):
                s = s + tap(pext_ref, c - j)
            for j in range(hi + 1, w - w // 2):
                s = s + tap(pext_ref, c + j)
            lo, hi = w // 2, w - w // 2 - 1
            sums.append(s)
        t = row + (i * tile + base)
        cnt = jnp.minimum(t + ahead, seq_len) - jnp.maximum(t - back, 0)
        total = sums[-1]
        for gi in range(len(POOL_WINDOWS) - 2, -1, -1):
            total = jnp.where(group == gi, sums[gi], total)
        pool_ref[base:base + MIX_ROWS, :] = total / cnt.astype(F32) - tap(pext_ref, c)

    ya = _silu(_layer_norm(conv_ref[...], clg_ref[...], clb_ref[...]))
    ya = _dot(ya.astype(BF16), pww_ref[...]) + pwb_ref[...]
    yc = _dot(pool_ref[...].astype(BF16), plw_ref[...]) * pls_ref[...]
    v = (yf_ref[0] + yb_ref[0]) * _silu(z_ref[0])
    yb = v * lax.rsqrt(jnp.mean(v * v, axis=-1, keepdims=True) + RMS_EPS) * ng_ref[...]
    mix = (_dot(ya.astype(BF16), wo_ref[0:D_CONV, :])
           + _dot(yb.astype(BF16), wo_ref[D_CONV:D_CONV + D_SSM, :])
           + _dot(yc.astype(BF16), wo_ref[D_CONV + D_SSM:, :]))
    o_ref[0] = _layer_norm(ALPHA * x_ref[0] + gate_ref[0] * mix, l1g_ref[...], l1b_ref[...])


def _mixers(a, up, y_f, y_b, z, x, gate, lp, tile):
    B, L, _ = x.shape
    hb = tile // HALO
    n_halo = L // HALO
    tok = lambda w: pl.BlockSpec((1, tile, w), lambda b, i: (b, i, 0))
    prev = lambda w: pl.BlockSpec((1, HALO, w), lambda b, i: (b, jnp.maximum(i * hb - 1, 0), 0))
    nxt = lambda w: pl.BlockSpec((1, HALO, w), lambda b, i: (b, jnp.minimum((i + 1) * hb, n_halo - 1), 0))
    const = lambda shape: pl.BlockSpec(shape, lambda b, i: (0,) * len(shape))
    return pl.pallas_call(
        functools.partial(_mix_kernel, tile=tile, seq_len=L),
        out_shape=jax.ShapeDtypeStruct((B, L, D_MODEL), F32),
        grid=(B, L // tile),
        in_specs=[
            tok(2 * D_CONV), prev(2 * D_CONV), nxt(2 * D_CONV),
            tok(D_POOL), prev(D_POOL), nxt(D_POOL),
            tok(D_SSM), tok(D_SSM), tok(D_SSM), tok(D_MODEL),
            pl.BlockSpec((1, 1, D_MODEL), lambda b, i: (b, 0, 0)),
            const((CONV_K, D_CONV)), const((1, D_CONV)), const((1, D_CONV)), const((1, D_CONV)),
            const((D_CONV, D_CONV)), const((1, D_CONV)), const((1, D_SSM)),
            const((D_POOL, D_POOL)), const((1, D_POOL)), const((D_MODEL, D_MODEL)),
            const((1, D_MODEL)), const((1, D_MODEL)),
        ],
        out_specs=tok(D_MODEL),
        scratch_shapes=[pltpu.VMEM((SUBLANES, tile + 2 * HALO, D_CONV), F32),
                        pltpu.VMEM((SUBLANES, tile + 2 * HALO, D_POOL), F32),
                        pltpu.VMEM((tile, D_CONV), F32), pltpu.VMEM((tile, D_POOL), F32)],
        compiler_params=_params(("arbitrary", "arbitrary")),
        name="mixers_outproj_ln1",
    )(a, a, a, up, up, up, y_f, y_b, z, x, gate,
      lp["conv_dw_w"], lp["conv_dw_b"], lp["conv_ln_g"], lp["conv_ln_b"], lp["conv_pw_w"], lp["conv_pw_b"],
      lp["ssm_norm_g"], lp["pool_w"], lp["pool_scale"], lp["w_out"], lp["ln1_g"], lp["ln1_b"])


def _ffn_kernel(*refs, rows, width, has_halo):
    n_x = 3 if has_halo else 1
    x_ref = refs[0]
    (sh_ref, sc_ref, gate_ref, wv_ref, wg_ref, wd_ref, dww_ref, dwb_ref, l2g_ref, l2b_ref,
     o_ref, h_ref, g0a_ref, g0b_ref, vala_ref, valb_ref, gm_ref, gp_ref, act_ref) = refs[n_x:]
    tile = rows * width
    ext_rows = rows + 2 if has_halo else rows
    first = width if has_halo else 0
    i = pl.program_id(1)
    n_tiles = pl.num_programs(1)

    def modulate(v):
        return v * (1.0 + sc_ref[0]) + sh_ref[0]

    h_ref[first:first + tile, :] = modulate(x_ref[0]).astype(BF16)
    if has_halo:
        xp_ref, xn_ref = refs[1], refs[2]
        h_ref[0:width, :] = (modulate(xp_ref[0]) * jnp.where(i > 0, 1.0, 0.0)).astype(BF16)
        h_ref[first + tile:, :] = (modulate(xn_ref[0]) * jnp.where(i < n_tiles - 1, 1.0, 0.0)).astype(BF16)
    tok = lax.broadcasted_iota(jnp.int32, (width, FF_CHUNK), 0)
    not_first = tok != 0
    not_last = tok != width - 1

    def up_project(j, g0_ref, val_ref):
        g0_ref[...] = _dot(h_ref[...], wg_ref[:, j * FF_CHUNK:(j + 1) * FF_CHUNK])
        val_ref[...] = _dot(h_ref[first:first + tile, :], wv_ref[:, j * FF_CHUNK:(j + 1) * FF_CHUNK])

    def conv_gate(j, g0_ref, val_ref):
        for e in range(ext_rows):
            blk = g0_ref[e * width:(e + 1) * width, :]
            gm_ref[e * width:(e + 1) * width, :] = jnp.where(not_first, pltpu.roll(blk, 1, 0), 0.0)
            gp_ref[e * width:(e + 1) * width, :] = jnp.where(not_last, pltpu.roll(blk, width - 1, 0), 0.0)
        cols = slice(j * FF_CHUNK, (j + 1) * FF_CHUNK)
        tap = lambda k: dww_ref[k:k + 1, cols]
        for r in range(rows):
            conv = dwb_ref[:, cols]
            for dr in range(FFN_K):
                e = r + dr - 1 + (1 if has_halo else 0)
                if e < 0 or e >= ext_rows:
                    continue
                sl = slice(e * width, (e + 1) * width)
                conv = (conv + tap(3 * dr) * gm_ref[sl, :] + tap(3 * dr + 1) * g0_ref[sl, :]
                        + tap(3 * dr + 2) * gp_ref[sl, :])
            out = slice(r * width, (r + 1) * width)
            act_ref[out, j * FF_CHUNK:(j + 1) * FF_CHUNK] = (val_ref[out, :] * jax.nn.gelu(conv)).astype(BF16)

    slots = ((g0a_ref, vala_ref), (g0b_ref, valb_ref))
    up_project(0, *slots[0])
    for j in range(N_FF_CHUNKS):
        if j + 1 < N_FF_CHUNKS:
            up_project(j + 1, *slots[(j + 1) % 2])
        conv_gate(j, *slots[j % 2])
    f = _dot(act_ref[...], wd_ref[...])
    o_ref[0] = _layer_norm(ALPHA * x_ref[0] + gate_ref[0] * f, l2g_ref[...], l2b_ref[...])


def _conv_ffn(x, shift, scale, gate, lp, rows, width):
    B, L, _ = x.shape
    tile = rows * width
    n_tiles = L // tile
    has_halo = n_tiles * rows > 1
    n_rows = L // width
    ext = (rows + 2) * width if has_halo else tile
    tok = pl.BlockSpec((1, tile, D_MODEL), lambda b, i: (b, i, 0))
    vec = pl.BlockSpec((1, 1, D_MODEL), lambda b, i: (b, 0, 0))
    const = lambda shape: pl.BlockSpec(shape, lambda b, i: (0,) * len(shape), pipeline_mode=pl.Buffered(1))
    x_specs, x_args = [tok], [x]
    if has_halo:
        x_specs += [pl.BlockSpec((1, width, D_MODEL), lambda b, i: (b, jnp.maximum(i * rows - 1, 0), 0)),
                    pl.BlockSpec((1, width, D_MODEL), lambda b, i: (b, jnp.minimum((i + 1) * rows, n_rows - 1), 0))]
        x_args += [x, x]
    return pl.pallas_call(
        functools.partial(_ffn_kernel, rows=rows, width=width, has_halo=has_halo),
        out_shape=jax.ShapeDtypeStruct((B, L, D_MODEL), F32),
        grid=(B, n_tiles),
        in_specs=x_specs + [
            vec, vec, vec,
            const((D_MODEL, D_FF)), const((D_MODEL, D_FF)), const((D_FF, D_MODEL)),
            const((FFN_K * FFN_K, D_FF)), const((1, D_FF)),
            const((1, D_MODEL)), const((1, D_MODEL)),
        ],
        out_specs=tok,
        scratch_shapes=[pltpu.VMEM((ext, D_MODEL), BF16),
                        pltpu.VMEM((ext, FF_CHUNK), F32), pltpu.VMEM((ext, FF_CHUNK), F32),
                        pltpu.VMEM((tile, FF_CHUNK), F32), pltpu.VMEM((tile, FF_CHUNK), F32),
                        pltpu.VMEM((ext, FF_CHUNK), F32), pltpu.VMEM((ext, FF_CHUNK), F32),
                        pltpu.VMEM((tile, D_FF), BF16)],
        compiler_params=_params(("arbitrary", "arbitrary")),
        name="conv_ffn_ln2",
    )(*x_args, shift, scale, gate, lp["w_up_val"], lp["w_up_gate"], lp["w_down"], lp["ffn_dw_w"], lp["ffn_dw_b"],
      lp["ln2_g"], lp["ln2_b"])


def _layer_params(l, w_in, conv_dw_w, conv_dw_b, conv_ln_g, conv_ln_b, conv_pw_w, conv_pw_b, ssm_conv_w, ssm_conv_b,
                  ssm_dt_bias, ssm_A_log, ssm_D, ssm_norm_g, pool_w, pool_scale, w_out, ln1_g, ln1_b, w_up, ffn_dw_w,
                  ffn_dw_b, w_down, ln2_g, ln2_b):
    row = lambda v: v.reshape(1, -1)
    wi = w_in[l]
    s0, s1, s2, s3, s4 = 2 * D_CONV, 2 * D_CONV + D_SSM, 2 * D_CONV + D_SSM + D_XBC, 2 * D_CONV + D_SSM + D_XBC + 2 * HEADS, wi.shape[1]
    w_in_r = jnp.concatenate([wi[:, :s2], wi[:, s3:s4], wi[:, s2:s3], jnp.zeros((D_MODEL, DT_PAD - 2 * HEADS), F32)],
                             axis=1).astype(BF16)
    pad_lanes = lambda v: jnp.pad(v.reshape(1, -1), ((0, 0), (0, DT_PAD - 2 * HEADS)))
    pool_bd = jnp.zeros((D_POOL, D_POOL), F32)
    for g in range(len(POOL_WINDOWS)):
        sl = slice(g * POOL_GROUP, (g + 1) * POOL_GROUP)
        pool_bd = pool_bd.at[sl, sl].set(pool_w[l, g])
    return {
        "w_in": w_in_r,
        "conv_dw_w": conv_dw_w[l], "conv_dw_b": row(conv_dw_b[l]), "conv_ln_g": row(conv_ln_g[l]),
        "conv_ln_b": row(conv_ln_b[l]), "conv_pw_w": conv_pw_w[l].astype(BF16), "conv_pw_b": row(conv_pw_b[l]),
        "ssm_conv_w": ssm_conv_w[l], "ssm_conv_b": ssm_conv_b[l].reshape(2, 1, D_XBC),
        "dt_bias": pad_lanes(ssm_dt_bias[l]), "a_log": pad_lanes(ssm_A_log[l]),
        "d_skip": jnp.repeat(ssm_D[l], HEAD_DIM, axis=1).reshape(2, 1, D_SSM),
        "ssm_norm_g": row(ssm_norm_g[l]), "pool_w": pool_bd.astype(BF16), "pool_scale": row(pool_scale[l]),
        "w_out": w_out[l].astype(BF16), "ln1_g": row(ln1_g[l]), "ln1_b": row(ln1_b[l]),
        "w_up_val": w_up[l, :, :D_FF].astype(BF16), "w_up_gate": w_up[l, :, D_FF:].astype(BF16),
        "w_down": w_down[l].astype(BF16),
        "ffn_dw_w": ffn_dw_w[l].reshape(FFN_K * FFN_K, D_FF), "ffn_dw_b": row(ffn_dw_b[l]),
        "ln2_g": row(ln2_g[l]), "ln2_b": row(ln2_b[l]),
    }


def _ssd_branch(h, shift, scale, lp, h0_f, h0_b, tile):
    a, z, xbc, up, dt = _in_projection(h, shift, scale, lp["w_in"], tile)
    y_f, y_b, st_f, st_b = _ssd_scan(xbc, dt, h0_f, h0_b, lp)
    return (a, up, y_f, y_b, z), st_f, st_b


def kernel(x, c, ctx, c_ctx, w_mod, b_mod, w_in, conv_dw_w, conv_dw_b, conv_ln_g, conv_ln_b, conv_pw_w, conv_pw_b,
           ssm_conv_w, ssm_conv_b, ssm_dt_bias, ssm_A_log, ssm_D, ssm_norm_g, pool_w, pool_scale, w_out, ln1_g, ln1_b,
           w_up, ffn_dw_w, ffn_dw_b, w_down, ln2_g, ln2_b):
    B, L, _ = x.shape
    ctx_len = ctx.shape[1]
    assert L % (8 * GRID_W) == 0 and ctx_len % SSD_CHUNK == 0 and B + 1 <= MOD_ROWS
    cvec = jnp.concatenate([c, c_ctx[None, :], jnp.zeros((MOD_ROWS - B - 1, D_MODEL), F32)], axis=0)
    mods = _modulation(cvec, w_mod, b_mod.reshape(DEPTH, 1, N_MOD * D_MODEL))
    zero_state = jnp.zeros((B, GROUPS, STATE, HEADS_PER_GROUP * HEAD_DIM), F32)
    lat_tile = 8 * GRID_W
    for l in range(DEPTH):
        last = l == DEPTH - 1
        lp = _layer_params(l, w_in, conv_dw_w, conv_dw_b, conv_ln_g, conv_ln_b, conv_pw_w, conv_pw_b, ssm_conv_w,
                           ssm_conv_b, ssm_dt_bias, ssm_A_log, ssm_D, ssm_norm_g, pool_w, pool_scale, w_out, ln1_g,
                           ln1_b, w_up, ffn_dw_w, ffn_dw_b, w_down, ln2_g, ln2_b)
        mx = [mods[l, :B, k * D_MODEL:(k + 1) * D_MODEL].reshape(B, 1, D_MODEL) for k in range(N_MOD)]
        mc = [jnp.broadcast_to(mods[l, B, k * D_MODEL:(k + 1) * D_MODEL].reshape(1, 1, D_MODEL), (B, 1, D_MODEL))
              for k in range(N_MOD)]
        (a, up, y_f, y_b, z), st_f, st_b = _ssd_branch(ctx, mc[0], mc[1], lp, zero_state, zero_state, ctx_len)
        if not last:
            ctx1 = _mixers(a, up, y_f, y_b, z, ctx, mc[2], lp, ctx_len)
            ctx = _conv_ffn(ctx1, mc[3], mc[4], mc[5], lp, 1, ctx_len)
        (a, up, y_f, y_b, z), _, _ = _ssd_branch(x, mx[0], mx[1], lp, st_f, st_b, lat_tile)
        x1 = _mixers(a, up, y_f, y_b, z, x, mx[2], lp, lat_tile)
        x = _conv_ffn(x1, mx[3], mx[4], mx[5], lp, 8, GRID_W)
    return x
```

```python
import functools

import numpy as np
import jax
import jax.numpy as jnp
from jax import lax
from jax.experimental import pallas as pl
from jax.experimental.pallas import tpu as pltpu

F32 = jnp.float32
BF16 = jnp.bfloat16

D_MODEL = 1024
DEPTH = 4
GRID_W = 64
D_CONV = 256
CONV_K = 31
D_SSM = 512
HEAD_DIM = 64
HEADS = 8
GROUPS = 2
HEADS_PER_GROUP = HEADS // GROUPS
STATE = 128
SSM_CONV_K = 4
D_XBC = D_SSM + 2 * GROUPS * STATE
D_POOL = 256
POOL_WINDOWS = (2, 4, 8, 16)
POOL_GROUP = D_POOL // len(POOL_WINDOWS)
D_FF = 2816
FFN_K = 3
N_MOD = 6
ALPHA = (2.0 * DEPTH) ** 0.25
LN_EPS = 1e-5
RMS_EPS = 1e-5

LANES = 128
SUBLANES = 8
HALO = 16
DT_PAD = LANES
IN_COLS_PAD = 2 * D_CONV + D_SSM + D_XBC + D_POOL + DT_PAD
SSD_CHUNK = 128
SSD_BLOCK = 512
FF_CHUNK = 256
N_FF_CHUNKS = D_FF // FF_CHUNK
FFN_ROWS = 8
DOWN_GROUP = 2
DOWN_PIECES_PER_STEP = 2
GELU_C0 = (2.0 / np.pi) ** 0.5
GELU_C1 = GELU_C0 * 0.044715
MOD_ROWS = 16
MOD_TILE = 1024
VMEM_LIMIT = 52 * 1024 * 1024


def _silu(v):
    return v * jax.nn.sigmoid(v)


def _dot(a, b):
    return jnp.dot(a, b, preferred_element_type=F32)


def _params(sem):
    return pltpu.CompilerParams(dimension_semantics=sem, vmem_limit_bytes=VMEM_LIMIT)


def _mod_kernel(c_ref, w_ref, b_ref, o_ref):
    s = _silu(c_ref[...])
    o_ref[0] = _dot(s.astype(BF16), w_ref[0].astype(BF16)) + b_ref[0]


def _modulation(cvec, w_mod, b_mod):
    n_cols = N_MOD * D_MODEL
    return pl.pallas_call(
        _mod_kernel,
        out_shape=jax.ShapeDtypeStruct((DEPTH, MOD_ROWS, n_cols), F32),
        grid=(DEPTH, n_cols // MOD_TILE),
        in_specs=[
            pl.BlockSpec((MOD_ROWS, D_MODEL), lambda l, j: (0, 0)),
            pl.BlockSpec((1, D_MODEL, MOD_TILE), lambda l, j: (l, 0, j)),
            pl.BlockSpec((1, 1, MOD_TILE), lambda l, j: (l, 0, j)),
        ],
        out_specs=pl.BlockSpec((1, MOD_ROWS, MOD_TILE), lambda l, j: (l, 0, j)),
        compiler_params=_params(("arbitrary", "arbitrary")),
        name="adaln_modulation",
    )(cvec, w_mod, b_mod)


_A0, _Z0, _X0, _P0, _T0 = 0, 2 * D_CONV, 2 * D_CONV + D_SSM, 2 * D_CONV + D_SSM + D_XBC, 2 * D_CONV + D_SSM + D_XBC + D_POOL


def _inproj_kernel(x_ref, sh_ref, sc_ref, w_ref, a_ref, z_ref, xbc_ref, up_ref, dt_ref):
    h = (x_ref[0] * (1.0 + sc_ref[0]) + sh_ref[0]).astype(BF16)
    a_ref[0] = _dot(h, w_ref[:, _A0:_Z0])
    z_ref[0] = _dot(h, w_ref[:, _Z0:_X0])
    xbc_ref[0] = _dot(h, w_ref[:, _X0:_P0])
    up_ref[0] = _dot(h, w_ref[:, _P0:_T0])
    dt_ref[0] = _dot(h, w_ref[:, _T0:IN_COLS_PAD])


def _layer_block(layer, shape, **kw):
    return pl.BlockSpec((None,) + shape, lambda b, i: (layer,) + (0,) * len(shape), **kw)


def _in_projection(x, shift, scale, lp, tile):
    B, L, _ = x.shape
    widths = (2 * D_CONV, D_SSM, D_XBC, D_POOL, DT_PAD)
    tok = lambda w: pl.BlockSpec((1, tile, w), lambda b, i: (b, i, 0))
    vec = pl.BlockSpec((1, 1, D_MODEL), lambda b, i: (b, 0, 0))
    return pl.pallas_call(
        _inproj_kernel,
        out_shape=[jax.ShapeDtypeStruct((B, L, w), F32) for w in widths],
        grid=(B, L // tile),
        in_specs=[tok(D_MODEL), vec, vec, _layer_block(lp["layer"], (D_MODEL, IN_COLS_PAD))],
        out_specs=[tok(w) for w in widths],
        compiler_params=_params(("arbitrary", "arbitrary")),
        name="in_projection",
    )(x, shift, scale, lp["w_in"])


def _split_bf16(v, parts):
    out = []
    for _ in range(parts - 1):
        p = v.astype(BF16)
        out.append(p)
        v = v - p.astype(F32)
    out.append(v.astype(BF16))
    return out


def _ssd_kernel(xf_ref, dtf_ref, xb_ref, dtb_ref, h0f_ref, h0b_ref, cw_ref, cb_ref, dtbias_ref, alog_ref, dskip_ref,
                tri_ref, expand_ref, yf_ref, yb_ref, stf_ref, stb_ref, extf_ref, extb_ref):
    @pl.when(pl.program_id(1) == 0)
    def _():
        stf_ref[...] = h0f_ref[...]
        stb_ref[...] = h0b_ref[...]
        extf_ref[...] = jnp.zeros_like(extf_ref)
        extb_ref[...] = jnp.zeros_like(extb_ref)

    n_sub = xf_ref.shape[1] // SSD_CHUNK
    for sub in range(n_sub):
        _ssd_chunk(xf_ref, dtf_ref, cw_ref, cb_ref, dtbias_ref, alog_ref, dskip_ref, tri_ref, expand_ref, yf_ref,
                   stf_ref, extf_ref, sub * SSD_CHUNK, 0)
        _ssd_chunk(xb_ref, dtb_ref, cw_ref, cb_ref, dtbias_ref, alog_ref, dskip_ref, tri_ref, expand_ref, yb_ref,
                   stb_ref, extb_ref, (n_sub - 1 - sub) * SSD_CHUNK, 1)


def _ssd_chunk(xbc_ref, dt_ref, cw_ref, cb_ref, dtb_ref, alog_ref, dskip_ref, tri_ref, expand_ref, y_ref, st_ref,
               ext_ref, r0, direction):
    Q = SSD_CHUNK
    reverse = direction == 1
    rows = slice(r0, r0 + Q)
    xbc = xbc_ref[0, rows, :]
    conv = cb_ref[direction]
    if not reverse:
        ext_ref[SUBLANES:SUBLANES + Q, :] = xbc
        for k in range(SSM_CONV_K):
            off = SUBLANES - (SSM_CONV_K - 1) + k
            conv = conv + cw_ref[direction, k:k + 1, :] * ext_ref[off:off + Q, :]
        ext_ref[0:SUBLANES, :] = xbc[Q - SUBLANES:Q, :]
    else:
        ext_ref[0:Q, :] = xbc
        for k in range(SSM_CONV_K):
            off = SSM_CONV_K - 1 - k
            conv = conv + cw_ref[direction, k:k + 1, :] * ext_ref[off:off + Q, :]
        ext_ref[Q:Q + SUBLANES, :] = xbc[0:SUBLANES, :]
    u = _silu(conv)
    xs = u[:, :D_SSM]
    b_bf = u[:, D_SSM:D_SSM + GROUPS * STATE].astype(BF16)
    c_bf = u[:, D_SSM + GROUPS * STATE:].astype(BF16)

    lane = lax.broadcasted_iota(jnp.int32, (1, LANES), 1)
    own = (lane >= direction * HEADS) & (lane < (direction + 1) * HEADS)
    a_head = jnp.where(own, -jnp.exp(alog_ref[...]), 0.0)
    dt = jax.nn.softplus(dt_ref[0, rows, :] + dtb_ref[...])
    a = dt * a_head
    acum = _dot(tri_ref[direction], jnp.concatenate(_split_bf16(a, 3), axis=0))
    acum_t = acum.T
    last = Q - 1 if not reverse else 0
    total = acum[last:last + 1, :]
    stacked = jnp.concatenate([dt, jnp.exp(acum), jnp.exp(total - acum)], axis=0)
    expanded = _dot(jnp.concatenate(_split_bf16(stacked, 2), axis=1), expand_ref[direction])
    dt_x, decay_in, decay_out = expanded[0:Q], expanded[Q:2 * Q], expanded[2 * Q:3 * Q]

    xdt = xs * dt_x
    xdt_bf = xdt.astype(BF16)
    xdec_bf = (xdt * decay_out).astype(BF16)
    row = lax.broadcasted_iota(jnp.int32, (Q, Q), 0)
    col = lax.broadcasted_iota(jnp.int32, (Q, Q), 1)
    mask = (row >= col) if not reverse else (row <= col)
    gw = HEADS_PER_GROUP * HEAD_DIM
    for g in range(GROUPS):
        cg = c_bf[:, g * STATE:(g + 1) * STATE]
        bg = b_bf[:, g * STATE:(g + 1) * STATE]
        scores = lax.dot_general(cg, bg, (((1,), (1,)), ((), ())), preferred_element_type=F32)
        state = st_ref[0, g]
        y_off = _dot(cg, state.astype(BF16))
        heads = []
        for hh in range(HEADS_PER_GROUP):
            h = g * HEADS_PER_GROUP + hh
            ln = direction * HEADS + h
            seg = jnp.where(mask, jnp.exp(acum[:, ln:ln + 1] - acum_t[ln:ln + 1, :]), 0.0)
            heads.append(_dot((scores * seg).astype(BF16), xdt_bf[:, h * HEAD_DIM:(h + 1) * HEAD_DIM]))
        y_diag = jnp.concatenate(heads, axis=1)
        sl = slice(g * gw, (g + 1) * gw)
        y_ref[0, rows, sl] = y_diag + y_off * decay_in[:, sl] + dskip_ref[direction, :, sl] * xs[:, sl]
        upd = lax.dot_general(bg, xdec_bf[:, sl], (((0,), (0,)), ((), ())), preferred_element_type=F32)
        st_ref[0, g] = state * decay_in[last:last + 1, sl] + upd


def _ssd_scan(xbc, dt, h0_f, h0_b, lp):
    B, L, _ = xbc.shape
    Q = SSD_CHUNK
    block = min(SSD_BLOCK, L)
    n_blocks = L // block
    tri = np.tril(np.ones((Q, Q), np.float32))
    tri3 = jnp.asarray(np.stack([np.concatenate([t, t, t], axis=1) for t in (tri, tri.T)]), BF16)
    expand = np.zeros((2, LANES, D_SSM), np.float32)
    for d in range(2):
        for h in range(HEADS):
            expand[d, d * HEADS + h, h * HEAD_DIM:(h + 1) * HEAD_DIM] = 1.0
    expand2 = jnp.asarray(np.concatenate([expand, expand], axis=1), BF16)
    gw = HEADS_PER_GROUP * HEAD_DIM
    const = lambda shape: pl.BlockSpec(shape, lambda b, i: (0,) * len(shape))
    fwd = lambda w: pl.BlockSpec((1, block, w), lambda b, i: (b, i, 0))
    bwd = lambda w: pl.BlockSpec((1, block, w), lambda b, i: (b, n_blocks - 1 - i, 0))
    state_spec = pl.BlockSpec((1, GROUPS, STATE, gw), lambda b, i: (b, 0, 0, 0))
    y_shape = jax.ShapeDtypeStruct((B, L, D_SSM), F32)
    st_shape = jax.ShapeDtypeStruct((B, GROUPS, STATE, gw), F32)
    return pl.pallas_call(
        _ssd_kernel,
        out_shape=[y_shape, y_shape, st_shape, st_shape],
        grid=(B, n_blocks),
        in_specs=[
            fwd(D_XBC), fwd(DT_PAD), bwd(D_XBC), bwd(DT_PAD), state_spec, state_spec,
            const((2, SSM_CONV_K, D_XBC)), const((2, 1, D_XBC)), const((1, DT_PAD)), const((1, DT_PAD)),
            const((2, 1, D_SSM)), const((2, Q, 3 * Q)), const((2, 2 * LANES, D_SSM)),
        ],
        out_specs=[fwd(D_SSM), bwd(D_SSM), state_spec, state_spec],
        scratch_shapes=[pltpu.VMEM((Q + 2 * SUBLANES, D_XBC), F32), pltpu.VMEM((Q + 2 * SUBLANES, D_XBC), F32)],
        compiler_params=_params(("arbitrary", "arbitrary")),
        name="ssd_scan",
    )(xbc, dt, xbc, dt, h0_f, h0_b, lp["ssm_conv_w"], lp["ssm_conv_b"], lp["dt_bias"], lp["a_log"], lp["d_skip"],
      tri3, expand2)


def _layer_norm(v, gain, bias):
    mu = jnp.mean(v, axis=-1, keepdims=True)
    d = v - mu
    var = jnp.mean(d * d, axis=-1, keepdims=True)
    return d * lax.rsqrt(var + LN_EPS) * gain + bias


MIX_ROWS = 64


def _mix_kernel(a_ref, ap_ref, an_ref, up_ref, upp_ref, upn_ref, yf_ref, yb_ref, z_ref, x_ref, gate_ref,
                dww_ref, dwb_ref, clg_ref, clb_ref, pww_ref, pwb_ref, ng_ref, plw_ref, pls_ref, wo_ref,
                l1g_ref, l1b_ref, o_ref, ext_ref, pext_ref, conv_ref, pool_ref, *, tile, seq_len):
    i = pl.program_id(1)
    n_tiles = seq_len // tile
    has_prev = jnp.where(i > 0, 1.0, 0.0)
    has_next = jnp.where(i < n_tiles - 1, 1.0, 0.0)

    def glu(v):
        return v[:, :D_CONV] * jax.nn.sigmoid(v[:, D_CONV:])

    ext_ref[0, 0:HALO, :] = glu(ap_ref[0]) * has_prev
    ext_ref[0, HALO:HALO + tile, :] = glu(a_ref[0])
    ext_ref[0, HALO + tile:, :] = glu(an_ref[0]) * has_next
    pext_ref[0, 0:HALO, :] = upp_ref[0] * has_prev
    pext_ref[0, HALO:HALO + tile, :] = up_ref[0]
    pext_ref[0, HALO + tile:, :] = upn_ref[0] * has_next
    n_shift = tile + 2 * HALO - SUBLANES
    for s in range(1, SUBLANES):
        ext_ref[s, 0:n_shift, :] = ext_ref[0, s:s + n_shift, :]
        pext_ref[s, 0:n_shift, :] = pext_ref[0, s:s + n_shift, :]

    def tap(ref, o):
        s = o % SUBLANES
        return ref[s, o - s:o - s + MIX_ROWS, :]

    lane = lax.broadcasted_iota(jnp.int32, (MIX_ROWS, D_POOL), 1)
    group = lane // POOL_GROUP
    row = lax.broadcasted_iota(jnp.int32, (MIX_ROWS, D_POOL), 0)
    back = jnp.zeros((MIX_ROWS, D_POOL), jnp.int32)
    ahead = jnp.zeros((MIX_ROWS, D_POOL), jnp.int32)
    for gi, w in enumerate(POOL_WINDOWS):
        back = jnp.where(group == gi, w // 2, back)
        ahead = jnp.where(group == gi, w - w // 2, ahead)
    half_c = CONV_K // 2
    for r in range(tile // MIX_ROWS):
        base = r * MIX_ROWS
        acc = dwb_ref[...] + dww_ref[0:1, :] * tap(ext_ref, base + HALO - half_c)
        for k in range(1, CONV_K):
            acc = acc + dww_ref[k:k + 1, :] * tap(ext_ref, base + HALO - half_c + k)
        conv_ref[base:base + MIX_ROWS, :] = acc
        c = base + HALO
        sums = []
        s = tap(pext_ref, c - 1) + tap(pext_ref, c)
        sums.append(s)
        lo, hi = 1, 0
        for w in POOL_WINDOWS[1:]:
            for j in range(w // 2, lo, -1):
                s = s + tap(pext_ref, c - j)
            for j in range(hi + 1, w - w // 2):
                s = s + tap(pext_ref, c + j)
            lo, hi = w // 2, w - w // 2 - 1
            sums.append(s)
        t = row + (i * tile + base)
        cnt = jnp.minimum(t + ahead, seq_len) - jnp.maximum(t - back, 0)
        total = sums[-1]
        for gi in range(len(POOL_WINDOWS) - 2, -1, -1):
            total = jnp.where(group == gi, sums[gi], total)
        pool_ref[base:base + MIX_ROWS, :] = total / cnt.astype(F32) - tap(pext_ref, c)

    ya = _silu(_layer_norm(conv_ref[...], clg_ref[...], clb_ref[...]))
    ya = _dot(ya.astype(BF16), pww_ref[...]) + pwb_ref[...]
    yc = _dot(pool_ref[...].astype(BF16), plw_ref[...]) * pls_ref[...]
    v = (yf_ref[0] + yb_ref[0]) * _silu(z_ref[0])
    yb = v * lax.rsqrt(jnp.mean(v * v, axis=-1, keepdims=True) + RMS_EPS) * ng_ref[...]
    mix = (_dot(ya.astype(BF16), wo_ref[0:D_CONV, :])
           + _dot(yb.astype(BF16), wo_ref[D_CONV:D_CONV + D_SSM, :])
           + _dot(yc.astype(BF16), wo_ref[D_CONV + D_SSM:, :]))
    o_ref[0] = _layer_norm(ALPHA * x_ref[0] + gate_ref[0] * mix, l1g_ref[...], l1b_ref[...])


def _mixers(a, up, y_f, y_b, z, x, gate, lp, tile):
    B, L, _ = x.shape
    hb = tile // HALO
    n_halo = L // HALO
    tok = lambda w: pl.BlockSpec((1, tile, w), lambda b, i: (b, i, 0))
    prev = lambda w: pl.BlockSpec((1, HALO, w), lambda b, i: (b, jnp.maximum(i * hb - 1, 0), 0))
    nxt = lambda w: pl.BlockSpec((1, HALO, w), lambda b, i: (b, jnp.minimum((i + 1) * hb, n_halo - 1), 0))
    const = lambda shape: pl.BlockSpec(shape, lambda b, i: (0,) * len(shape))
    return pl.pallas_call(
        functools.partial(_mix_kernel, tile=tile, seq_len=L),
        out_shape=jax.ShapeDtypeStruct((B, L, D_MODEL), F32),
        grid=(B, L // tile),
        in_specs=[
            tok(2 * D_CONV), prev(2 * D_CONV), nxt(2 * D_CONV),
            tok(D_POOL), prev(D_POOL), nxt(D_POOL),
            tok(D_SSM), tok(D_SSM), tok(D_SSM), tok(D_MODEL),
            pl.BlockSpec((1, 1, D_MODEL), lambda b, i: (b, 0, 0)),
            const((CONV_K, D_CONV)), const((1, D_CONV)), const((1, D_CONV)), const((1, D_CONV)),
            const((D_CONV, D_CONV)), const((1, D_CONV)), const((1, D_SSM)),
            const((D_POOL, D_POOL)), const((1, D_POOL)), _layer_block(lp["layer"], (D_MODEL, D_MODEL)),
            const((1, D_MODEL)), const((1, D_MODEL)),
        ],
        out_specs=tok(D_MODEL),
        scratch_shapes=[pltpu.VMEM((SUBLANES, tile + 2 * HALO, D_CONV), F32),
                        pltpu.VMEM((SUBLANES, tile + 2 * HALO, D_POOL), F32),
                        pltpu.VMEM((tile, D_CONV), F32), pltpu.VMEM((tile, D_POOL), F32)],
        compiler_params=_params(("arbitrary", "arbitrary")),
        name="mixers_outproj_ln1",
    )(a, a, a, up, up, up, y_f, y_b, z, x, gate,
      lp["conv_dw_w"], lp["conv_dw_b"], lp["conv_ln_g"], lp["conv_ln_b"], lp["conv_pw_w"], lp["conv_pw_b"],
      lp["ssm_norm_g"], lp["pool_w"], lp["pool_scale"], lp["w_out"], lp["ln1_g"], lp["ln1_b"])


def _ffn_kernel(*refs, rows, width, has_halo):
    n_x = 3 if has_halo else 1
    x_ref = refs[0]
    (sh_ref, sc_ref, gate_ref, wu_ref, wd_ref, dww_ref, dwb_ref, l2g_ref, l2b_ref,
     o_ref, h_ref, g0a_ref, g0b_ref, vala_ref, valb_ref, gm_ref, gp_ref, act_ref, acc_ref) = refs[n_x:]
    tile = rows * width
    ext_rows = rows + 2 if has_halo else rows
    first = width if has_halo else 0
    i = pl.program_id(1)
    n_tiles = pl.num_programs(1)

    def modulate(v):
        return v * (1.0 + sc_ref[0]) + sh_ref[0]

    h_ref[first:first + tile, :] = modulate(x_ref[0]).astype(BF16)
    if has_halo:
        xp_ref, xn_ref = refs[1], refs[2]
        h_ref[0:width, :] = (modulate(xp_ref[0]) * jnp.where(i > 0, 1.0, 0.0)).astype(BF16)
        h_ref[first + tile:, :] = (modulate(xn_ref[0]) * jnp.where(i < n_tiles - 1, 1.0, 0.0)).astype(BF16)
    tok = lax.broadcasted_iota(jnp.int32, (width, FF_CHUNK), 0)
    not_first = tok != 0
    not_last = tok != width - 1

    def up_project(j, g0_ref, val_ref):
        g0_ref[...] = _dot(h_ref[...], wu_ref[:, D_FF + j * FF_CHUNK:D_FF + (j + 1) * FF_CHUNK])
        val_ref[...] = _dot(h_ref[first:first + tile, :], wu_ref[:, j * FF_CHUNK:(j + 1) * FF_CHUNK])

    def conv_gate(j, g0_ref, val_ref):
        for e in range(ext_rows):
            blk = g0_ref[e * width:(e + 1) * width, :]
            gm_ref[e * width:(e + 1) * width, :] = jnp.where(not_first, pltpu.roll(blk, 1, 0), 0.0)
            gp_ref[e * width:(e + 1) * width, :] = jnp.where(not_last, pltpu.roll(blk, width - 1, 0), 0.0)
        cols = slice(j * FF_CHUNK, (j + 1) * FF_CHUNK)
        tap = lambda k: dww_ref[k:k + 1, cols]
        for r in range(rows):
            conv = dwb_ref[:, cols]
            for dr in range(FFN_K):
                e = r + dr - 1 + (1 if has_halo else 0)
                if e < 0 or e >= ext_rows:
                    continue
                sl = slice(e * width, (e + 1) * width)
                conv = (conv + tap(3 * dr) * gm_ref[sl, :] + tap(3 * dr + 1) * g0_ref[sl, :]
                        + tap(3 * dr + 2) * gp_ref[sl, :])
            out = slice(r * width, (r + 1) * width)
            inner = conv * (GELU_C0 + GELU_C1 * (conv * conv))
            act = (val_ref[out, :] * conv) * (0.5 * jnp.tanh(inner) + 0.5)
            act_ref[out, j * FF_CHUNK:(j + 1) * FF_CHUNK] = act.astype(BF16)

    pieces = [(c0, min(c0 + DOWN_GROUP, N_FF_CHUNKS), n)
              for c0 in range(0, N_FF_CHUNKS, DOWN_GROUP) for n in range(D_MODEL // FF_CHUNK)]

    def down_project(ready, budget):
        while pieces and budget > 0 and pieces[0][1] <= ready:
            c0, c1, n = pieces.pop(0)
            k_rows = slice(c0 * FF_CHUNK, c1 * FF_CHUNK)
            n_cols = slice(n * FF_CHUNK, (n + 1) * FF_CHUNK)
            part = _dot(act_ref[:, k_rows], wd_ref[k_rows, n_cols])
            if c0 == 0:
                acc_ref[:, n_cols] = part
            else:
                acc_ref[:, n_cols] += part
            budget -= 1

    slots = ((g0a_ref, vala_ref), (g0b_ref, valb_ref))
    up_project(0, *slots[0])
    for j in range(N_FF_CHUNKS):
        if j + 1 < N_FF_CHUNKS:
            up_project(j + 1, *slots[(j + 1) % 2])
        down_project(j, DOWN_PIECES_PER_STEP)
        conv_gate(j, *slots[j % 2])
    down_project(N_FF_CHUNKS, len(pieces))
    o_ref[0] = _layer_norm(ALPHA * x_ref[0] + gate_ref[0] * acc_ref[...], l2g_ref[...], l2b_ref[...])


def _conv_ffn(x, shift, scale, gate, lp, rows, width):
    B, L, _ = x.shape
    tile = rows * width
    n_tiles = L // tile
    has_halo = n_tiles * rows > 1
    n_rows = L // width
    ext = (rows + 2) * width if has_halo else tile
    tok = pl.BlockSpec((1, tile, D_MODEL), lambda b, i: (b, i, 0))
    vec = pl.BlockSpec((1, 1, D_MODEL), lambda b, i: (b, 0, 0))
    const = lambda shape: pl.BlockSpec(shape, lambda b, i: (0,) * len(shape), pipeline_mode=pl.Buffered(1))
    x_specs, x_args = [tok], [x]
    if has_halo:
        x_specs += [pl.BlockSpec((1, width, D_MODEL), lambda b, i: (b, jnp.maximum(i * rows - 1, 0), 0)),
                    pl.BlockSpec((1, width, D_MODEL), lambda b, i: (b, jnp.minimum((i + 1) * rows, n_rows - 1), 0))]
        x_args += [x, x]
    return pl.pallas_call(
        functools.partial(_ffn_kernel, rows=rows, width=width, has_halo=has_halo),
        out_shape=jax.ShapeDtypeStruct((B, L, D_MODEL), F32),
        grid=(B, n_tiles),
        in_specs=x_specs + [
            vec, vec, vec,
            _layer_block(lp["layer"], (D_MODEL, 2 * D_FF), pipeline_mode=pl.Buffered(1)),
            _layer_block(lp["layer"], (D_FF, D_MODEL), pipeline_mode=pl.Buffered(1)),
            const((FFN_K * FFN_K, D_FF)), const((1, D_FF)),
            const((1, D_MODEL)), const((1, D_MODEL)),
        ],
        out_specs=tok,
        scratch_shapes=[pltpu.VMEM((ext, D_MODEL), BF16),
                        pltpu.VMEM((ext, FF_CHUNK), F32), pltpu.VMEM((ext, FF_CHUNK), F32),
                        pltpu.VMEM((tile, FF_CHUNK), F32), pltpu.VMEM((tile, FF_CHUNK), F32),
                        pltpu.VMEM((ext, FF_CHUNK), F32), pltpu.VMEM((ext, FF_CHUNK), F32),
                        pltpu.VMEM((tile, D_FF), BF16), pltpu.VMEM((tile, D_MODEL), F32)],
        compiler_params=_params(("arbitrary", "arbitrary")),
        name="conv_ffn_ln2",
    )(*x_args, shift, scale, gate, lp["w_up"], lp["w_down"], lp["ffn_dw_w"], lp["ffn_dw_b"], lp["ln2_g"], lp["ln2_b"])


def _stacked_matmul_weights(w_in, w_out, w_up, w_down):
    s2 = 2 * D_CONV + D_SSM + D_XBC
    s3 = s2 + 2 * HEADS
    w_in_r = jnp.concatenate([w_in[:, :, :s2], w_in[:, :, s3:], w_in[:, :, s2:s3],
                              jnp.zeros((DEPTH, D_MODEL, DT_PAD - 2 * HEADS), F32)], axis=2).astype(BF16)
    return {"w_in": w_in_r, "w_out": w_out.astype(BF16), "w_up": w_up.astype(BF16), "w_down": w_down.astype(BF16)}


def _layer_params(l, stacked, conv_dw_w, conv_dw_b, conv_ln_g, conv_ln_b, conv_pw_w, conv_pw_b, ssm_conv_w, ssm_conv_b,
                  ssm_dt_bias, ssm_A_log, ssm_D, ssm_norm_g, pool_w, pool_scale, ln1_g, ln1_b, ffn_dw_w, ffn_dw_b,
                  ln2_g, ln2_b):
    row = lambda v: v.reshape(1, -1)
    pad_lanes = lambda v: jnp.pad(v.reshape(1, -1), ((0, 0), (0, DT_PAD - 2 * HEADS)))
    pool_bd = jnp.zeros((D_POOL, D_POOL), F32)
    for g in range(len(POOL_WINDOWS)):
        sl = slice(g * POOL_GROUP, (g + 1) * POOL_GROUP)
        pool_bd = pool_bd.at[sl, sl].set(pool_w[l, g])
    return {
        "layer": l, **stacked,
        "conv_dw_w": conv_dw_w[l], "conv_dw_b": row(conv_dw_b[l]), "conv_ln_g": row(conv_ln_g[l]),
        "conv_ln_b": row(conv_ln_b[l]), "conv_pw_w": conv_pw_w[l].astype(BF16), "conv_pw_b": row(conv_pw_b[l]),
        "ssm_conv_w": ssm_conv_w[l], "ssm_conv_b": ssm_conv_b[l].reshape(2, 1, D_XBC),
        "dt_bias": pad_lanes(ssm_dt_bias[l]), "a_log": pad_lanes(ssm_A_log[l]),
        "d_skip": jnp.repeat(ssm_D[l], HEAD_DIM, axis=1).reshape(2, 1, D_SSM),
        "ssm_norm_g": row(ssm_norm_g[l]), "pool_w": pool_bd.astype(BF16), "pool_scale": row(pool_scale[l]),
        "ln1_g": row(ln1_g[l]), "ln1_b": row(ln1_b[l]),
        "ffn_dw_w": ffn_dw_w[l].reshape(FFN_K * FFN_K, D_FF), "ffn_dw_b": row(ffn_dw_b[l]),
        "ln2_g": row(ln2_g[l]), "ln2_b": row(ln2_b[l]),
    }


def _ssd_branch(h, shift, scale, lp, h0_f, h0_b, tile):
    a, z, xbc, up, dt = _in_projection(h, shift, scale, lp, tile)
    y_f, y_b, st_f, st_b = _ssd_scan(xbc, dt, h0_f, h0_b, lp)
    return (a, up, y_f, y_b, z), st_f, st_b


def kernel(x, c, ctx, c_ctx, w_mod, b_mod, w_in, conv_dw_w, conv_dw_b, conv_ln_g, conv_ln_b, conv_pw_w, conv_pw_b,
           ssm_conv_w, ssm_conv_b, ssm_dt_bias, ssm_A_log, ssm_D, ssm_norm_g, pool_w, pool_scale, w_out, ln1_g, ln1_b,
           w_up, ffn_dw_w, ffn_dw_b, w_down, ln2_g, ln2_b):
    B, L, _ = x.shape
    ctx_len = ctx.shape[1]
    assert L % (8 * GRID_W) == 0 and ctx_len % SSD_CHUNK == 0 and B + 1 <= MOD_ROWS
    cvec = jnp.concatenate([c, c_ctx[None, :], jnp.zeros((MOD_ROWS - B - 1, D_MODEL), F32)], axis=0)
    mods = _modulation(cvec, w_mod, b_mod.reshape(DEPTH, 1, N_MOD * D_MODEL))
    zero_state = jnp.zeros((B, GROUPS, STATE, HEADS_PER_GROUP * HEAD_DIM), F32)
    lat_tile = 8 * GRID_W
    stacked = _stacked_matmul_weights(w_in, w_out, w_up, w_down)
    for l in range(DEPTH):
        last = l == DEPTH - 1
        lp = _layer_params(l, stacked, conv_dw_w, conv_dw_b, conv_ln_g, conv_ln_b, conv_pw_w, conv_pw_b, ssm_conv_w,
                           ssm_conv_b, ssm_dt_bias, ssm_A_log, ssm_D, ssm_norm_g, pool_w, pool_scale, ln1_g, ln1_b,
                           ffn_dw_w, ffn_dw_b, ln2_g, ln2_b)
        mx = [mods[l, :B, k * D_MODEL:(k + 1) * D_MODEL].reshape(B, 1, D_MODEL) for k in range(N_MOD)]
        mc = [jnp.broadcast_to(mods[l, B, k * D_MODEL:(k + 1) * D_MODEL].reshape(1, 1, D_MODEL), (B, 1, D_MODEL))
              for k in range(N_MOD)]
        (a, up, y_f, y_b, z), st_f, st_b = _ssd_branch(ctx, mc[0], mc[1], lp, zero_state, zero_state, ctx_len)
        if not last:
            ctx1 = _mixers(a, up, y_f, y_b, z, ctx, mc[2], lp, ctx_len)
            ctx = _conv_ffn(ctx1, mc[3], mc[4], mc[5], lp, 1, ctx_len)
        (a, up, y_f, y_b, z), _, _ = _ssd_branch(x, mx[0], mx[1], lp, st_f, st_b, lat_tile)
        x1 = _mixers(a, up, y_f, y_b, z, x, mx[2], lp, lat_tile)
        x = _conv_ffn(x1, mx[3], mx[4], mx[5], lp, FFN_ROWS, GRID_W)
    return x
```

```python
import functools

import numpy as np
import jax
import jax.numpy as jnp
from jax import lax
from jax.experimental import pallas as pl
from jax.experimental.pallas import tpu as pltpu

F32 = jnp.float32
BF16 = jnp.bfloat16

D_MODEL = 1024
DEPTH = 4
GRID_W = 64
D_CONV = 256
CONV_K = 31
D_SSM = 512
HEAD_DIM = 64
HEADS = 8
GROUPS = 2
HEADS_PER_GROUP = HEADS // GROUPS
STATE = 128
SSM_CONV_K = 4
D_XBC = D_SSM + 2 * GROUPS * STATE
D_POOL = 256
POOL_WINDOWS = (2, 4, 8, 16)
POOL_GROUP = D_POOL // len(POOL_WINDOWS)
D_FF = 2816
FFN_K = 3
N_MOD = 6
ALPHA = (2.0 * DEPTH) ** 0.25
LN_EPS = 1e-5
RMS_EPS = 1e-5

LANES = 128
SUBLANES = 8
HALO = 16
DT_PAD = LANES
IN_COLS_PAD = 2 * D_CONV + D_SSM + D_XBC + D_POOL + DT_PAD
SSD_CHUNK = 128
SSD_BLOCK = 1024
INPROJ_TILE = 1024
FF_CHUNK = 256
N_FF_CHUNKS = D_FF // FF_CHUNK
FFN_ROWS = 8
DOWN_GROUP = 2
DOWN_PIECES_PER_STEP = 2
GELU_C0 = (2.0 / np.pi) ** 0.5
GELU_C1 = GELU_C0 * 0.044715
MOD_ROWS = 16
MOD_TILE = 1024
VMEM_LIMIT = 52 * 1024 * 1024


def _silu(v):
    return v * jax.nn.sigmoid(v)


def _dot(a, b):
    return jnp.dot(a, b, preferred_element_type=F32)


def _params(sem):
    return pltpu.CompilerParams(dimension_semantics=sem, vmem_limit_bytes=VMEM_LIMIT)


def _mod_kernel(c_ref, w_ref, b_ref, o_ref):
    s = _silu(c_ref[...])
    o_ref[0] = _dot(s.astype(BF16), w_ref[0].astype(BF16)) + b_ref[0]


def _modulation(cvec, w_mod, b_mod):
    n_cols = N_MOD * D_MODEL
    return pl.pallas_call(
        _mod_kernel,
        out_shape=jax.ShapeDtypeStruct((DEPTH, MOD_ROWS, n_cols), F32),
        grid=(DEPTH, n_cols // MOD_TILE),
        in_specs=[
            pl.BlockSpec((MOD_ROWS, D_MODEL), lambda l, j: (0, 0)),
            pl.BlockSpec((1, D_MODEL, MOD_TILE), lambda l, j: (l, 0, j)),
            pl.BlockSpec((1, 1, MOD_TILE), lambda l, j: (l, 0, j)),
        ],
        out_specs=pl.BlockSpec((1, MOD_ROWS, MOD_TILE), lambda l, j: (l, 0, j)),
        compiler_params=_params(("arbitrary", "arbitrary")),
        name="adaln_modulation",
    )(cvec, w_mod, b_mod)


_A0, _Z0, _X0, _P0, _T0 = 0, 2 * D_CONV, 2 * D_CONV + D_SSM, 2 * D_CONV + D_SSM + D_XBC, 2 * D_CONV + D_SSM + D_XBC + D_POOL


def _inproj_kernel(x_ref, sh_ref, sc_ref, w_ref, a_ref, z_ref, xbc_ref, up_ref, dt_ref):
    h = (x_ref[0] * (1.0 + sc_ref[0]) + sh_ref[0]).astype(BF16)
    a_ref[0] = _dot(h, w_ref[:, _A0:_Z0])
    z_ref[0] = _dot(h, w_ref[:, _Z0:_X0])
    xbc_ref[0] = _dot(h, w_ref[:, _X0:_P0])
    up_ref[0] = _dot(h, w_ref[:, _P0:_T0])
    dt_ref[0] = _dot(h, w_ref[:, _T0:IN_COLS_PAD])


def _layer_block(layer, shape, **kw):
    return pl.BlockSpec((None,) + shape, lambda b, i: (layer,) + (0,) * len(shape), **kw)


def _in_projection(x, shift, scale, lp, tile):
    B, L, _ = x.shape
    widths = (2 * D_CONV, D_SSM, D_XBC, D_POOL, DT_PAD)
    tok = lambda w: pl.BlockSpec((1, tile, w), lambda b, i: (b, i, 0))
    vec = pl.BlockSpec((1, 1, D_MODEL), lambda b, i: (b, 0, 0))
    return pl.pallas_call(
        _inproj_kernel,
        out_shape=[jax.ShapeDtypeStruct((B, L, w), F32) for w in widths],
        grid=(B, L // tile),
        in_specs=[tok(D_MODEL), vec, vec, _layer_block(lp["layer"], (D_MODEL, IN_COLS_PAD))],
        out_specs=[tok(w) for w in widths],
        compiler_params=_params(("arbitrary", "arbitrary")),
        name="in_projection",
    )(x, shift, scale, lp["w_in"])


def _split_bf16(v, parts):
    out = []
    for _ in range(parts - 1):
        p = v.astype(BF16)
        out.append(p)
        v = v - p.astype(F32)
    out.append(v.astype(BF16))
    return out


def _ssd_kernel(xf_ref, dtf_ref, xb_ref, dtb_ref, h0f_ref, h0b_ref, cw_ref, cb_ref, dtbias_ref, alog_ref, dskip_ref,
                tri_ref, expand_ref, yf_ref, yb_ref, stf_ref, stb_ref, extf_ref, extb_ref):
    @pl.when(pl.program_id(1) == 0)
    def _():
        stf_ref[...] = h0f_ref[...]
        stb_ref[...] = h0b_ref[...]
        extf_ref[...] = jnp.zeros_like(extf_ref)
        extb_ref[...] = jnp.zeros_like(extb_ref)

    n_sub = xf_ref.shape[1] // SSD_CHUNK
    for sub in range(n_sub):
        _ssd_chunk(xf_ref, dtf_ref, cw_ref, cb_ref, dtbias_ref, alog_ref, dskip_ref, tri_ref, expand_ref, yf_ref,
                   stf_ref, extf_ref, sub * SSD_CHUNK, 0)
        _ssd_chunk(xb_ref, dtb_ref, cw_ref, cb_ref, dtbias_ref, alog_ref, dskip_ref, tri_ref, expand_ref, yb_ref,
                   stb_ref, extb_ref, (n_sub - 1 - sub) * SSD_CHUNK, 1)


def _ssd_chunk(xbc_ref, dt_ref, cw_ref, cb_ref, dtb_ref, alog_ref, dskip_ref, tri_ref, expand_ref, y_ref, st_ref,
               ext_ref, r0, direction):
    Q = SSD_CHUNK
    reverse = direction == 1
    rows = slice(r0, r0 + Q)
    xbc = xbc_ref[0, rows, :]
    n_ext = Q + SUBLANES
    w = [cw_ref[direction, k:k + 1, :] for k in range(SSM_CONV_K)]
    if not reverse:
        ext_ref[SUBLANES:n_ext, :] = xbc
        xe = ext_ref[...]
        xn = pltpu.roll(xe, 1, 0)
        near = w[3] * xe + w[2] * xn
        far = pltpu.roll(w[1] * xe + w[0] * xn, 2, 0)
        conv = (near + far)[SUBLANES:n_ext, :] + cb_ref[direction]
        ext_ref[0:SUBLANES, :] = xbc[Q - SUBLANES:Q, :]
    else:
        ext_ref[0:Q, :] = xbc
        xe = ext_ref[...]
        xn = pltpu.roll(xe, n_ext - 1, 0)
        near = w[3] * xe + w[2] * xn
        far = pltpu.roll(w[1] * xe + w[0] * xn, n_ext - 2, 0)
        conv = (near + far)[0:Q, :] + cb_ref[direction]
        ext_ref[Q:n_ext, :] = xbc[0:SUBLANES, :]
    u = _silu(conv)
    xs = u[:, :D_SSM]
    b_bf = u[:, D_SSM:D_SSM + GROUPS * STATE].astype(BF16)
    c_bf = u[:, D_SSM + GROUPS * STATE:].astype(BF16)

    lane = lax.broadcasted_iota(jnp.int32, (1, LANES), 1)
    own = (lane >= direction * HEADS) & (lane < (direction + 1) * HEADS)
    a_head = jnp.where(own, -jnp.exp(alog_ref[...]), 0.0)
    dt = jax.nn.softplus(dt_ref[0, rows, :] + dtb_ref[...])
    a = dt * a_head
    acum = _dot(tri_ref[direction], jnp.concatenate(_split_bf16(a, 3), axis=0))
    acum_t = acum.T
    last = Q - 1 if not reverse else 0
    total = acum[last:last + 1, :]
    stacked = jnp.concatenate([dt, jnp.exp(acum), jnp.exp(total - acum)], axis=0)
    expanded = _dot(jnp.concatenate(_split_bf16(stacked, 2), axis=1), expand_ref[direction])
    dt_x, decay_in, decay_out = expanded[0:Q], expanded[Q:2 * Q], expanded[2 * Q:3 * Q]

    xdt = xs * dt_x
    xdt_bf = xdt.astype(BF16)
    xdec_bf = (xdt * decay_out).astype(BF16)
    row = lax.broadcasted_iota(jnp.int32, (Q, Q), 0)
    col = lax.broadcasted_iota(jnp.int32, (Q, Q), 1)
    mask = (row >= col) if not reverse else (row <= col)
    gw = HEADS_PER_GROUP * HEAD_DIM
    for g in range(GROUPS):
        cg = c_bf[:, g * STATE:(g + 1) * STATE]
        bg = b_bf[:, g * STATE:(g + 1) * STATE]
        scores = lax.dot_general(cg, bg, (((1,), (1,)), ((), ())), preferred_element_type=F32)
        state = st_ref[0, g]
        y_off = _dot(cg, state.astype(BF16))
        heads = []
        for hh in range(HEADS_PER_GROUP):
            h = g * HEADS_PER_GROUP + hh
            ln = direction * HEADS + h
            seg = jnp.where(mask, jnp.exp(acum[:, ln:ln + 1] - acum_t[ln:ln + 1, :]), 0.0)
            heads.append(_dot((scores * seg).astype(BF16), xdt_bf[:, h * HEAD_DIM:(h + 1) * HEAD_DIM]))
        y_diag = jnp.concatenate(heads, axis=1)
        sl = slice(g * gw, (g + 1) * gw)
        y_ref[0, rows, sl] = y_diag + y_off * decay_in[:, sl] + dskip_ref[direction, :, sl] * xs[:, sl]
        upd = lax.dot_general(bg, xdec_bf[:, sl], (((0,), (0,)), ((), ())), preferred_element_type=F32)
        st_ref[0, g] = state * decay_in[last:last + 1, sl] + upd


def _ssd_scan(xbc, dt, h0_f, h0_b, lp):
    B, L, _ = xbc.shape
    Q = SSD_CHUNK
    block = min(SSD_BLOCK, L)
    n_blocks = L // block
    tri = np.tril(np.ones((Q, Q), np.float32))
    tri3 = jnp.asarray(np.stack([np.concatenate([t, t, t], axis=1) for t in (tri, tri.T)]), BF16)
    expand = np.zeros((2, LANES, D_SSM), np.float32)
    for d in range(2):
        for h in range(HEADS):
            expand[d, d * HEADS + h, h * HEAD_DIM:(h + 1) * HEAD_DIM] = 1.0
    expand2 = jnp.asarray(np.concatenate([expand, expand], axis=1), BF16)
    gw = HEADS_PER_GROUP * HEAD_DIM
    const = lambda shape: pl.BlockSpec(shape, lambda b, i: (0,) * len(shape))
    fwd = lambda w: pl.BlockSpec((1, block, w), lambda b, i: (b, i, 0))
    bwd = lambda w: pl.BlockSpec((1, block, w), lambda b, i: (b, n_blocks - 1 - i, 0))
    state_spec = pl.BlockSpec((1, GROUPS, STATE, gw), lambda b, i: (b, 0, 0, 0))
    y_shape = jax.ShapeDtypeStruct((B, L, D_SSM), F32)
    st_shape = jax.ShapeDtypeStruct((B, GROUPS, STATE, gw), F32)
    return pl.pallas_call(
        _ssd_kernel,
        out_shape=[y_shape, y_shape, st_shape, st_shape],
        grid=(B, n_blocks),
        in_specs=[
            fwd(D_XBC), fwd(DT_PAD), bwd(D_XBC), bwd(DT_PAD), state_spec, state_spec,
            const((2, SSM_CONV_K, D_XBC)), const((2, 1, D_XBC)), const((1, DT_PAD)), const((1, DT_PAD)),
            const((2, 1, D_SSM)), const((2, Q, 3 * Q)), const((2, 2 * LANES, D_SSM)),
        ],
        out_specs=[fwd(D_SSM), bwd(D_SSM), state_spec, state_spec],
        scratch_shapes=[pltpu.VMEM((Q + SUBLANES, D_XBC), F32), pltpu.VMEM((Q + SUBLANES, D_XBC), F32)],
        compiler_params=_params(("arbitrary", "arbitrary")),
        name="ssd_scan",
    )(xbc, dt, xbc, dt, h0_f, h0_b, lp["ssm_conv_w"], lp["ssm_conv_b"], lp["dt_bias"], lp["a_log"], lp["d_skip"],
      tri3, expand2)


def _layer_norm(v, gain, bias):
    mu = jnp.mean(v, axis=-1, keepdims=True)
    d = v - mu
    var = jnp.mean(d * d, axis=-1, keepdims=True)
    return d * lax.rsqrt(var + LN_EPS) * gain + bias


MIX_ROWS = 64


def _mix_kernel(a_ref, ap_ref, an_ref, up_ref, upp_ref, upn_ref, yf_ref, yb_ref, z_ref, x_ref, gate_ref,
                dww_ref, dwb_ref, clg_ref, clb_ref, pww_ref, pwb_ref, ng_ref, plw_ref, pls_ref, wo_ref,
                l1g_ref, l1b_ref, o_ref, ext_ref, pext_ref, conv_ref, pool_ref, *, tile, seq_len):
    i = pl.program_id(1)
    n_tiles = seq_len // tile
    has_prev = jnp.where(i > 0, 1.0, 0.0)
    has_next = jnp.where(i < n_tiles - 1, 1.0, 0.0)

    def glu(v):
        return v[:, :D_CONV] * jax.nn.sigmoid(v[:, D_CONV:])

    ext_ref[0, 0:HALO, :] = glu(ap_ref[0]) * has_prev
    ext_ref[0, HALO:HALO + tile, :] = glu(a_ref[0])
    ext_ref[0, HALO + tile:, :] = glu(an_ref[0]) * has_next
    pext_ref[0, 0:HALO, :] = upp_ref[0] * has_prev
    pext_ref[0, HALO:HALO + tile, :] = up_ref[0]
    pext_ref[0, HALO + tile:, :] = upn_ref[0] * has_next
    n_ext = tile + 2 * HALO
    u_ext = ext_ref[0]
    p_ext = pext_ref[0]
    for s in range(1, SUBLANES):
        ext_ref[s] = pltpu.roll(u_ext, n_ext - s, 0)
        pext_ref[s] = pltpu.roll(p_ext, n_ext - s, 0)

    def tap(ref, o):
        s = o % SUBLANES
        return ref[s, o - s:o - s + MIX_ROWS, :]

    lane = lax.broadcasted_iota(jnp.int32, (MIX_ROWS, D_POOL), 1)
    group = lane // POOL_GROUP
    row = lax.broadcasted_iota(jnp.int32, (MIX_ROWS, D_POOL), 0)
    back = jnp.zeros((MIX_ROWS, D_POOL), jnp.int32)
    ahead = jnp.zeros((MIX_ROWS, D_POOL), jnp.int32)
    for gi, w in enumerate(POOL_WINDOWS):
        back = jnp.where(group == gi, w // 2, back)
        ahead = jnp.where(group == gi, w - w // 2, ahead)
    half_c = CONV_K // 2
    for r in range(tile // MIX_ROWS):
        base = r * MIX_ROWS
        acc = dwb_ref[...] + dww_ref[0:1, :] * tap(ext_ref, base + HALO - half_c)
        for k in range(1, CONV_K):
            acc = acc + dww_ref[k:k + 1, :] * tap(ext_ref, base + HALO - half_c + k)
        conv_ref[base:base + MIX_ROWS, :] = acc
        c = base + HALO
        sums = []
        s = tap(pext_ref, c - 1) + tap(pext_ref, c)
        sums.append(s)
        lo, hi = 1, 0
        for w in POOL_WINDOWS[1:]:
            for j in range(w // 2, lo, -1):
                s = s + tap(pext_ref, c - j)
            for j in range(hi + 1, w - w // 2):
                s = s + tap(pext_ref, c + j)
            lo, hi = w // 2, w - w // 2 - 1
            sums.append(s)
        t = row + (i * tile + base)
        cnt = jnp.minimum(t + ahead, seq_len) - jnp.maximum(t - back, 0)
        total = sums[-1]
        for gi in range(len(POOL_WINDOWS) - 2, -1, -1):
            total = jnp.where(group == gi, sums[gi], total)
        pool_ref[base:base + MIX_ROWS, :] = total / cnt.astype(F32) - tap(pext_ref, c)

    ya = _silu(_layer_norm(conv_ref[...], clg_ref[...], clb_ref[...]))
    ya = _dot(ya.astype(BF16), pww_ref[...]) + pwb_ref[...]
    yc = _dot(pool_ref[...].astype(BF16), plw_ref[...]) * pls_ref[...]
    v = (yf_ref[0] + yb_ref[0]) * _silu(z_ref[0])
    yb = v * lax.rsqrt(jnp.mean(v * v, axis=-1, keepdims=True) + RMS_EPS) * ng_ref[...]
    mix = (_dot(ya.astype(BF16), wo_ref[0:D_CONV, :])
           + _dot(yb.astype(BF16), wo_ref[D_CONV:D_CONV + D_SSM, :])
           + _dot(yc.astype(BF16), wo_ref[D_CONV + D_SSM:, :]))
    o_ref[0] = _layer_norm(ALPHA * x_ref[0] + gate_ref[0] * mix, l1g_ref[...], l1b_ref[...])


def _mixers(a, up, y_f, y_b, z, x, gate, lp, tile):
    B, L, _ = x.shape
    hb = tile // HALO
    n_halo = L // HALO
    tok = lambda w: pl.BlockSpec((1, tile, w), lambda b, i: (b, i, 0))
    prev = lambda w: pl.BlockSpec((1, HALO, w), lambda b, i: (b, jnp.maximum(i * hb - 1, 0), 0))
    nxt = lambda w: pl.BlockSpec((1, HALO, w), lambda b, i: (b, jnp.minimum((i + 1) * hb, n_halo - 1), 0))
    const = lambda shape: pl.BlockSpec(shape, lambda b, i: (0,) * len(shape))
    return pl.pallas_call(
        functools.partial(_mix_kernel, tile=tile, seq_len=L),
        out_shape=jax.ShapeDtypeStruct((B, L, D_MODEL), F32),
        grid=(B, L // tile),
        in_specs=[
            tok(2 * D_CONV), prev(2 * D_CONV), nxt(2 * D_CONV),
            tok(D_POOL), prev(D_POOL), nxt(D_POOL),
            tok(D_SSM), tok(D_SSM), tok(D_SSM), tok(D_MODEL),
            pl.BlockSpec((1, 1, D_MODEL), lambda b, i: (b, 0, 0)),
            const((CONV_K, D_CONV)), const((1, D_CONV)), const((1, D_CONV)), const((1, D_CONV)),
            const((D_CONV, D_CONV)), const((1, D_CONV)), const((1, D_SSM)),
            const((D_POOL, D_POOL)), const((1, D_POOL)), _layer_block(lp["layer"], (D_MODEL, D_MODEL)),
            const((1, D_MODEL)), const((1, D_MODEL)),
        ],
        out_specs=tok(D_MODEL),
        scratch_shapes=[pltpu.VMEM((SUBLANES, tile + 2 * HALO, D_CONV), F32),
                        pltpu.VMEM((SUBLANES, tile + 2 * HALO, D_POOL), F32),
                        pltpu.VMEM((tile, D_CONV), F32), pltpu.VMEM((tile, D_POOL), F32)],
        compiler_params=_params(("arbitrary", "arbitrary")),
        name="mixers_outproj_ln1",
    )(a, a, a, up, up, up, y_f, y_b, z, x, gate,
      lp["conv_dw_w"], lp["conv_dw_b"], lp["conv_ln_g"], lp["conv_ln_b"], lp["conv_pw_w"], lp["conv_pw_b"],
      lp["ssm_norm_g"], lp["pool_w"], lp["pool_scale"], lp["w_out"], lp["ln1_g"], lp["ln1_b"])


def _ffn_kernel(*refs, rows, width, has_halo):
    n_x = 3 if has_halo else 1
    x_ref = refs[0]
    (sh_ref, sc_ref, gate_ref, wu_ref, wd_ref, dww_ref, dwb_ref, l2g_ref, l2b_ref,
     o_ref, h_ref, g0a_ref, g0b_ref, vala_ref, valb_ref, gm_ref, gp_ref, act_ref, acc_ref) = refs[n_x:]
    tile = rows * width
    ext_rows = rows + 2 if has_halo else rows
    first = width if has_halo else 0
    i = pl.program_id(1)
    n_tiles = pl.num_programs(1)

    def modulate(v):
        return v * (1.0 + sc_ref[0]) + sh_ref[0]

    h_ref[first:first + tile, :] = modulate(x_ref[0]).astype(BF16)
    if has_halo:
        xp_ref, xn_ref = refs[1], refs[2]
        h_ref[0:width, :] = (modulate(xp_ref[0]) * jnp.where(i > 0, 1.0, 0.0)).astype(BF16)
        h_ref[first + tile:, :] = (modulate(xn_ref[0]) * jnp.where(i < n_tiles - 1, 1.0, 0.0)).astype(BF16)
    tok = lax.broadcasted_iota(jnp.int32, (width, FF_CHUNK), 0)
    not_first = tok != 0
    not_last = tok != width - 1

    def up_project(j, g0_ref, val_ref):
        g0_ref[...] = _dot(h_ref[...], wu_ref[:, D_FF + j * FF_CHUNK:D_FF + (j + 1) * FF_CHUNK])
        val_ref[...] = _dot(h_ref[first:first + tile, :], wu_ref[:, j * FF_CHUNK:(j + 1) * FF_CHUNK])

    def conv_gate(j, g0_ref, val_ref):
        for e in range(ext_rows):
            blk = g0_ref[e * width:(e + 1) * width, :]
            gm_ref[e * width:(e + 1) * width, :] = jnp.where(not_first, pltpu.roll(blk, 1, 0), 0.0)
            gp_ref[e * width:(e + 1) * width, :] = jnp.where(not_last, pltpu.roll(blk, width - 1, 0), 0.0)
        cols = slice(j * FF_CHUNK, (j + 1) * FF_CHUNK)
        tap = lambda k: dww_ref[k:k + 1, cols]
        for r in range(rows):
            conv = dwb_ref[:, cols]
            for dr in range(FFN_K):
                e = r + dr - 1 + (1 if has_halo else 0)
                if e < 0 or e >= ext_rows:
                    continue
                sl = slice(e * width, (e + 1) * width)
                conv = (conv + tap(3 * dr) * gm_ref[sl, :] + tap(3 * dr + 1) * g0_ref[sl, :]
                        + tap(3 * dr + 2) * gp_ref[sl, :])
            out = slice(r * width, (r + 1) * width)
            inner = conv * (GELU_C0 + GELU_C1 * (conv * conv))
            act = (val_ref[out, :] * conv) * (0.5 * jnp.tanh(inner) + 0.5)
            act_ref[out, j * FF_CHUNK:(j + 1) * FF_CHUNK] = act.astype(BF16)

    pieces = [(c0, min(c0 + DOWN_GROUP, N_FF_CHUNKS), n)
              for c0 in range(0, N_FF_CHUNKS, DOWN_GROUP) for n in range(D_MODEL // FF_CHUNK)]

    def down_project(ready, budget):
        while pieces and budget > 0 and pieces[0][1] <= ready:
            c0, c1, n = pieces.pop(0)
            k_rows = slice(c0 * FF_CHUNK, c1 * FF_CHUNK)
            n_cols = slice(n * FF_CHUNK, (n + 1) * FF_CHUNK)
            part = _dot(act_ref[:, k_rows], wd_ref[k_rows, n_cols])
            if c0 == 0:
                acc_ref[:, n_cols] = part
            else:
                acc_ref[:, n_cols] += part
            budget -= 1

    slots = ((g0a_ref, vala_ref), (g0b_ref, valb_ref))
    up_project(0, *slots[0])
    for j in range(N_FF_CHUNKS):
        if j + 1 < N_FF_CHUNKS:
            up_project(j + 1, *slots[(j + 1) % 2])
        down_project(j, DOWN_PIECES_PER_STEP)
        conv_gate(j, *slots[j % 2])
    down_project(N_FF_CHUNKS, len(pieces))
    o_ref[0] = _layer_norm(ALPHA * x_ref[0] + gate_ref[0] * acc_ref[...], l2g_ref[...], l2b_ref[...])


def _conv_ffn(x, shift, scale, gate, lp, rows, width):
    B, L, _ = x.shape
    tile = rows * width
    n_tiles = L // tile
    has_halo = n_tiles * rows > 1
    n_rows = L // width
    ext = (rows + 2) * width if has_halo else tile
    tok = pl.BlockSpec((1, tile, D_MODEL), lambda b, i: (b, i, 0))
    vec = pl.BlockSpec((1, 1, D_MODEL), lambda b, i: (b, 0, 0))
    const = lambda shape: pl.BlockSpec(shape, lambda b, i: (0,) * len(shape), pipeline_mode=pl.Buffered(1))
    x_specs, x_args = [tok], [x]
    if has_halo:
        x_specs += [pl.BlockSpec((1, width, D_MODEL), lambda b, i: (b, jnp.maximum(i * rows - 1, 0), 0)),
                    pl.BlockSpec((1, width, D_MODEL), lambda b, i: (b, jnp.minimum((i + 1) * rows, n_rows - 1), 0))]
        x_args += [x, x]
    return pl.pallas_call(
        functools.partial(_ffn_kernel, rows=rows, width=width, has_halo=has_halo),
        out_shape=jax.ShapeDtypeStruct((B, L, D_MODEL), F32),
        grid=(B, n_tiles),
        in_specs=x_specs + [
            vec, vec, vec,
            _layer_block(lp["layer"], (D_MODEL, 2 * D_FF), pipeline_mode=pl.Buffered(1)),
            _layer_block(lp["layer"], (D_FF, D_MODEL), pipeline_mode=pl.Buffered(1)),
            const((FFN_K * FFN_K, D_FF)), const((1, D_FF)),
            const((1, D_MODEL)), const((1, D_MODEL)),
        ],
        out_specs=tok,
        scratch_shapes=[pltpu.VMEM((ext, D_MODEL), BF16),
                        pltpu.VMEM((ext, FF_CHUNK), F32), pltpu.VMEM((ext, FF_CHUNK), F32),
                        pltpu.VMEM((tile, FF_CHUNK), F32), pltpu.VMEM((tile, FF_CHUNK), F32),
                        pltpu.VMEM((ext, FF_CHUNK), F32), pltpu.VMEM((ext, FF_CHUNK), F32),
                        pltpu.VMEM((tile, D_FF), BF16), pltpu.VMEM((tile, D_MODEL), F32)],
        compiler_params=_params(("arbitrary", "arbitrary")),
        name="conv_ffn_ln2",
    )(*x_args, shift, scale, gate, lp["w_up"], lp["w_down"], lp["ffn_dw_w"], lp["ffn_dw_b"], lp["ln2_g"], lp["ln2_b"])


def _stacked_matmul_weights(w_in, w_out, w_up, w_down):
    s2 = 2 * D_CONV + D_SSM + D_XBC
    s3 = s2 + 2 * HEADS
    w_in_r = jnp.concatenate([w_in[:, :, :s2], w_in[:, :, s3:], w_in[:, :, s2:s3],
                              jnp.zeros((DEPTH, D_MODEL, DT_PAD - 2 * HEADS), F32)], axis=2).astype(BF16)
    return {"w_in": w_in_r, "w_out": w_out.astype(BF16), "w_up": w_up.astype(BF16), "w_down": w_down.astype(BF16)}


def _layer_params(l, stacked, conv_dw_w, conv_dw_b, conv_ln_g, conv_ln_b, conv_pw_w, conv_pw_b, ssm_conv_w, ssm_conv_b,
                  ssm_dt_bias, ssm_A_log, ssm_D, ssm_norm_g, pool_w, pool_scale, ln1_g, ln1_b, ffn_dw_w, ffn_dw_b,
                  ln2_g, ln2_b):
    row = lambda v: v.reshape(1, -1)
    pad_lanes = lambda v: jnp.pad(v.reshape(1, -1), ((0, 0), (0, DT_PAD - 2 * HEADS)))
    pool_bd = jnp.zeros((D_POOL, D_POOL), F32)
    for g in range(len(POOL_WINDOWS)):
        sl = slice(g * POOL_GROUP, (g + 1) * POOL_GROUP)
        pool_bd = pool_bd.at[sl, sl].set(pool_w[l, g])
    return {
        "layer": l, **stacked,
        "conv_dw_w": conv_dw_w[l], "conv_dw_b": row(conv_dw_b[l]), "conv_ln_g": row(conv_ln_g[l]),
        "conv_ln_b": row(conv_ln_b[l]), "conv_pw_w": conv_pw_w[l].astype(BF16), "conv_pw_b": row(conv_pw_b[l]),
        "ssm_conv_w": ssm_conv_w[l], "ssm_conv_b": ssm_conv_b[l].reshape(2, 1, D_XBC),
        "dt_bias": pad_lanes(ssm_dt_bias[l]), "a_log": pad_lanes(ssm_A_log[l]),
        "d_skip": jnp.repeat(ssm_D[l], HEAD_DIM, axis=1).reshape(2, 1, D_SSM),
        "ssm_norm_g": row(ssm_norm_g[l]), "pool_w": pool_bd.astype(BF16), "pool_scale": row(pool_scale[l]),
        "ln1_g": row(ln1_g[l]), "ln1_b": row(ln1_b[l]),
        "ffn_dw_w": ffn_dw_w[l].reshape(FFN_K * FFN_K, D_FF), "ffn_dw_b": row(ffn_dw_b[l]),
        "ln2_g": row(ln2_g[l]), "ln2_b": row(ln2_b[l]),
    }


def _ssd_branch(h, shift, scale, lp, h0_f, h0_b, tile):
    a, z, xbc, up, dt = _in_projection(h, shift, scale, lp, tile)
    y_f, y_b, st_f, st_b = _ssd_scan(xbc, dt, h0_f, h0_b, lp)
    return (a, up, y_f, y_b, z), st_f, st_b


def kernel(x, c, ctx, c_ctx, w_mod, b_mod, w_in, conv_dw_w, conv_dw_b, conv_ln_g, conv_ln_b, conv_pw_w, conv_pw_b,
           ssm_conv_w, ssm_conv_b, ssm_dt_bias, ssm_A_log, ssm_D, ssm_norm_g, pool_w, pool_scale, w_out, ln1_g, ln1_b,
           w_up, ffn_dw_w, ffn_dw_b, w_down, ln2_g, ln2_b):
    B, L, _ = x.shape
    ctx_len = ctx.shape[1]
    assert L % INPROJ_TILE == 0 and L % SSD_BLOCK == 0 and L % (FFN_ROWS * GRID_W) == 0
    assert ctx_len % SSD_CHUNK == 0 and B + 1 <= MOD_ROWS
    cvec = jnp.concatenate([c, c_ctx[None, :], jnp.zeros((MOD_ROWS - B - 1, D_MODEL), F32)], axis=0)
    mods = _modulation(cvec, w_mod, b_mod.reshape(DEPTH, 1, N_MOD * D_MODEL))
    zero_state = jnp.zeros((B, GROUPS, STATE, HEADS_PER_GROUP * HEAD_DIM), F32)
    lat_tile = 8 * GRID_W
    stacked = _stacked_matmul_weights(w_in, w_out, w_up, w_down)
    for l in range(DEPTH):
        last = l == DEPTH - 1
        lp = _layer_params(l, stacked, conv_dw_w, conv_dw_b, conv_ln_g, conv_ln_b, conv_pw_w, conv_pw_b, ssm_conv_w,
                           ssm_conv_b, ssm_dt_bias, ssm_A_log, ssm_D, ssm_norm_g, pool_w, pool_scale, ln1_g, ln1_b,
                           ffn_dw_w, ffn_dw_b, ln2_g, ln2_b)
        mx = [mods[l, :B, k * D_MODEL:(k + 1) * D_MODEL].reshape(B, 1, D_MODEL) for k in range(N_MOD)]
        mc = [jnp.broadcast_to(mods[l, B, k * D_MODEL:(k + 1) * D_MODEL].reshape(1, 1, D_MODEL), (B, 1, D_MODEL))
              for k in range(N_MOD)]
        (a, up, y_f, y_b, z), st_f, st_b = _ssd_branch(ctx, mc[0], mc[1], lp, zero_state, zero_state, ctx_len)
        if not last:
            ctx1 = _mixers(a, up, y_f, y_b, z, ctx, mc[2], lp, ctx_len)
            ctx = _conv_ffn(ctx1, mc[3], mc[4], mc[5], lp, 1, ctx_len)
        (a, up, y_f, y_b, z), _, _ = _ssd_branch(x, mx[0], mx[1], lp, st_f, st_b, INPROJ_TILE)
        x1 = _mixers(a, up, y_f, y_b, z, x, mx[2], lp, lat_tile)
        x = _conv_ffn(x1, mx[3], mx[4], mx[5], lp, FFN_ROWS, GRID_W)
    return x
```

```python
import functools

import numpy as np
import jax
import jax.numpy as jnp
from jax import lax
from jax.experimental import pallas as pl
from jax.experimental.pallas import tpu as pltpu

F32 = jnp.float32
BF16 = jnp.bfloat16

D_MODEL = 1024
DEPTH = 4
GRID_W = 64
D_CONV = 256
CONV_K = 31
D_SSM = 512
HEAD_DIM = 64
HEADS = 8
GROUPS = 2
HEADS_PER_GROUP = HEADS // GROUPS
STATE = 128
SSM_CONV_K = 4
D_XBC = D_SSM + 2 * GROUPS * STATE
D_POOL = 256
POOL_WINDOWS = (2, 4, 8, 16)
POOL_GROUP = D_POOL // len(POOL_WINDOWS)
D_FF = 2816
FFN_K = 3
N_MOD = 6
ALPHA = (2.0 * DEPTH) ** 0.25
LN_EPS = 1e-5
RMS_EPS = 1e-5

LANES = 128
SUBLANES = 8
HALO = 16
DT_PAD = LANES
IN_COLS_PAD = 2 * D_CONV + D_SSM + D_XBC + D_POOL + DT_PAD
SSD_CHUNK = 128
SSD_BLOCK = 1024
INPROJ_TILE = 1024
FF_CHUNK = 256
N_FF_CHUNKS = D_FF // FF_CHUNK
FFN_ROWS = 8
DOWN_GROUP = 2
DOWN_PIECES_PER_STEP = 2
LOG2_E = float(np.log2(np.e))
GELU_C0 = (2.0 / np.pi) ** 0.5
GELU_C1 = GELU_C0 * 0.044715
MOD_ROWS = 16
MOD_TILE = 1024
VMEM_LIMIT = 52 * 1024 * 1024


def _silu(v):
    return v * jax.nn.sigmoid(v)


def _dot(a, b):
    return jnp.dot(a, b, preferred_element_type=F32)


def _params(sem):
    return pltpu.CompilerParams(dimension_semantics=sem, vmem_limit_bytes=VMEM_LIMIT)


def _mod_kernel(c_ref, w_ref, b_ref, o_ref):
    s = _silu(c_ref[...])
    o_ref[0] = _dot(s.astype(BF16), w_ref[0].astype(BF16)) + b_ref[0]


def _modulation(cvec, w_mod, b_mod):
    n_cols = N_MOD * D_MODEL
    return pl.pallas_call(
        _mod_kernel,
        out_shape=jax.ShapeDtypeStruct((DEPTH, MOD_ROWS, n_cols), F32),
        grid=(DEPTH, n_cols // MOD_TILE),
        in_specs=[
            pl.BlockSpec((MOD_ROWS, D_MODEL), lambda l, j: (0, 0)),
            pl.BlockSpec((1, D_MODEL, MOD_TILE), lambda l, j: (l, 0, j)),
            pl.BlockSpec((1, 1, MOD_TILE), lambda l, j: (l, 0, j)),
        ],
        out_specs=pl.BlockSpec((1, MOD_ROWS, MOD_TILE), lambda l, j: (l, 0, j)),
        compiler_params=_params(("arbitrary", "arbitrary")),
        name="adaln_modulation",
    )(cvec, w_mod, b_mod)


_A0, _Z0, _X0, _P0, _T0 = 0, 2 * D_CONV, 2 * D_CONV + D_SSM, 2 * D_CONV + D_SSM + D_XBC, 2 * D_CONV + D_SSM + D_XBC + D_POOL


def _inproj_kernel(x_ref, sh_ref, sc_ref, w_ref, a_ref, z_ref, xbc_ref, up_ref, dt_ref):
    h = (x_ref[0] * (1.0 + sc_ref[0]) + sh_ref[0]).astype(BF16)
    a_ref[0] = _dot(h, w_ref[:, _A0:_Z0])
    z_ref[0] = _dot(h, w_ref[:, _Z0:_X0])
    xbc_ref[0] = _dot(h, w_ref[:, _X0:_P0])
    up_ref[0] = _dot(h, w_ref[:, _P0:_T0])
    dt_ref[0] = _dot(h, w_ref[:, _T0:IN_COLS_PAD])


def _layer_block(layer, shape, **kw):
    return pl.BlockSpec((None,) + shape, lambda b, i: (layer,) + (0,) * len(shape), **kw)


def _in_projection(x, shift, scale, lp, tile):
    B, L, _ = x.shape
    widths = (2 * D_CONV, D_SSM, D_XBC, D_POOL, DT_PAD)
    tok = lambda w: pl.BlockSpec((1, tile, w), lambda b, i: (b, i, 0))
    vec = pl.BlockSpec((1, 1, D_MODEL), lambda b, i: (b, 0, 0))
    return pl.pallas_call(
        _inproj_kernel,
        out_shape=[jax.ShapeDtypeStruct((B, L, w), F32) for w in widths],
        grid=(B, L // tile),
        in_specs=[tok(D_MODEL), vec, vec, _layer_block(lp["layer"], (D_MODEL, IN_COLS_PAD))],
        out_specs=[tok(w) for w in widths],
        compiler_params=_params(("arbitrary", "arbitrary")),
        name="in_projection",
    )(x, shift, scale, lp["w_in"])


def _split_bf16(v, parts):
    out = []
    for _ in range(parts - 1):
        p = v.astype(BF16)
        out.append(p)
        v = v - p.astype(F32)
    out.append(v.astype(BF16))
    return out


def _ssd_kernel(xf_ref, dtf_ref, xb_ref, dtb_ref, h0f_ref, h0b_ref, cw_ref, cb_ref, dtbias_ref, alog_ref, dskip_ref,
                tri_ref, expand_ref, yf_ref, yb_ref, stf_ref, stb_ref, extf_ref, extb_ref):
    @pl.when(pl.program_id(1) == 0)
    def _():
        stf_ref[...] = h0f_ref[...]
        stb_ref[...] = h0b_ref[...]
        extf_ref[...] = jnp.zeros_like(extf_ref)
        extb_ref[...] = jnp.zeros_like(extb_ref)

    n_sub = xf_ref.shape[1] // SSD_CHUNK
    for sub in range(n_sub):
        _ssd_chunk(xf_ref, dtf_ref, cw_ref, cb_ref, dtbias_ref, alog_ref, dskip_ref, tri_ref, expand_ref, yf_ref,
                   stf_ref, extf_ref, sub * SSD_CHUNK, 0)
        _ssd_chunk(xb_ref, dtb_ref, cw_ref, cb_ref, dtbias_ref, alog_ref, dskip_ref, tri_ref, expand_ref, yb_ref,
                   stb_ref, extb_ref, (n_sub - 1 - sub) * SSD_CHUNK, 1)


def _ssd_chunk(xbc_ref, dt_ref, cw_ref, cb_ref, dtb_ref, alog_ref, dskip_ref, tri_ref, expand_ref, y_ref, st_ref,
               ext_ref, r0, direction):
    Q = SSD_CHUNK
    reverse = direction == 1
    rows = slice(r0, r0 + Q)
    xbc = xbc_ref[0, rows, :]
    n_ext = Q + SUBLANES
    w = [cw_ref[direction, k:k + 1, :] for k in range(SSM_CONV_K)]
    if not reverse:
        ext_ref[SUBLANES:n_ext, :] = xbc
        xe = ext_ref[...]
        xn = pltpu.roll(xe, 1, 0)
        near = w[3] * xe + w[2] * xn
        far = pltpu.roll(w[1] * xe + w[0] * xn, 2, 0)
        conv = (near + far)[SUBLANES:n_ext, :] + cb_ref[direction]
        ext_ref[0:SUBLANES, :] = xbc[Q - SUBLANES:Q, :]
    else:
        ext_ref[0:Q, :] = xbc
        xe = ext_ref[...]
        xn = pltpu.roll(xe, n_ext - 1, 0)
        near = w[3] * xe + w[2] * xn
        far = pltpu.roll(w[1] * xe + w[0] * xn, n_ext - 2, 0)
        conv = (near + far)[0:Q, :] + cb_ref[direction]
        ext_ref[Q:n_ext, :] = xbc[0:SUBLANES, :]
    u = _silu(conv)
    xs = u[:, :D_SSM]
    b_bf = u[:, D_SSM:D_SSM + GROUPS * STATE].astype(BF16)
    c_bf = u[:, D_SSM + GROUPS * STATE:].astype(BF16)

    lane = lax.broadcasted_iota(jnp.int32, (1, LANES), 1)
    own = (lane >= direction * HEADS) & (lane < (direction + 1) * HEADS)
    a_head = jnp.where(own, -jnp.exp(alog_ref[...]), 0.0)
    dt = jax.nn.softplus(dt_ref[0, rows, :] + dtb_ref[...])
    a = dt * a_head
    acum = _dot(tri_ref[direction], jnp.concatenate(_split_bf16(a, 3), axis=0)) * LOG2_E
    acum_t = acum.T
    last = Q - 1 if not reverse else 0
    total = acum[last:last + 1, :]
    stacked = jnp.concatenate([dt, jnp.exp2(acum), jnp.exp2(total - acum)], axis=0)
    expanded = _dot(jnp.concatenate(_split_bf16(stacked, 2), axis=1), expand_ref[direction])
    dt_x, decay_in, decay_out = expanded[0:Q], expanded[Q:2 * Q], expanded[2 * Q:3 * Q]

    xdt = xs * dt_x
    xdt_bf = xdt.astype(BF16)
    xdec_bf = (xdt * decay_out).astype(BF16)
    row = lax.broadcasted_iota(jnp.int32, (Q, Q), 0)
    col = lax.broadcasted_iota(jnp.int32, (Q, Q), 1)
    mask = (row >= col) if not reverse else (row <= col)
    gw = HEADS_PER_GROUP * HEAD_DIM
    for g in range(GROUPS):
        cg = c_bf[:, g * STATE:(g + 1) * STATE]
        bg = b_bf[:, g * STATE:(g + 1) * STATE]
        scores = lax.dot_general(cg, bg, (((1,), (1,)), ((), ())), preferred_element_type=F32)
        state = st_ref[0, g]
        y_off = _dot(cg, state.astype(BF16))
        heads = []
        for hh in range(HEADS_PER_GROUP):
            h = g * HEADS_PER_GROUP + hh
            ln = direction * HEADS + h
            seg = jnp.where(mask, jnp.exp2(acum[:, ln:ln + 1] - acum_t[ln:ln + 1, :]), 0.0)
            heads.append(_dot((scores * seg).astype(BF16), xdt_bf[:, h * HEAD_DIM:(h + 1) * HEAD_DIM]))
        y_diag = jnp.concatenate(heads, axis=1)
        sl = slice(g * gw, (g + 1) * gw)
        y_ref[0, rows, sl] = y_diag + y_off * decay_in[:, sl] + dskip_ref[direction, :, sl] * xs[:, sl]
        upd = lax.dot_general(bg, xdec_bf[:, sl], (((0,), (0,)), ((), ())), preferred_element_type=F32)
        st_ref[0, g] = state * decay_in[last:last + 1, sl] + upd


def _ssd_scan(xbc, dt, h0_f, h0_b, lp):
    B, L, _ = xbc.shape
    Q = SSD_CHUNK
    block = min(SSD_BLOCK, L)
    n_blocks = L // block
    tri = np.tril(np.ones((Q, Q), np.float32))
    tri3 = jnp.asarray(np.stack([np.concatenate([t, t, t], axis=1) for t in (tri, tri.T)]), BF16)
    expand = np.zeros((2, LANES, D_SSM), np.float32)
    for d in range(2):
        for h in range(HEADS):
            expand[d, d * HEADS + h, h * HEAD_DIM:(h + 1) * HEAD_DIM] = 1.0
    expand2 = jnp.asarray(np.concatenate([expand, expand], axis=1), BF16)
    gw = HEADS_PER_GROUP * HEAD_DIM
    const = lambda shape: pl.BlockSpec(shape, lambda b, i: (0,) * len(shape))
    fwd = lambda w: pl.BlockSpec((1, block, w), lambda b, i: (b, i, 0))
    bwd = lambda w: pl.BlockSpec((1, block, w), lambda b, i: (b, n_blocks - 1 - i, 0))
    state_spec = pl.BlockSpec((1, GROUPS, STATE, gw), lambda b, i: (b, 0, 0, 0))
    y_shape = jax.ShapeDtypeStruct((B, L, D_SSM), F32)
    st_shape = jax.ShapeDtypeStruct((B, GROUPS, STATE, gw), F32)
    return pl.pallas_call(
        _ssd_kernel,
        out_shape=[y_shape, y_shape, st_shape, st_shape],
        grid=(B, n_blocks),
        in_specs=[
            fwd(D_XBC), fwd(DT_PAD), bwd(D_XBC), bwd(DT_PAD), state_spec, state_spec,
            const((2, SSM_CONV_K, D_XBC)), const((2, 1, D_XBC)), const((1, DT_PAD)), const((1, DT_PAD)),
            const((2, 1, D_SSM)), const((2, Q, 3 * Q)), const((2, 2 * LANES, D_SSM)),
        ],
        out_specs=[fwd(D_SSM), bwd(D_SSM), state_spec, state_spec],
        scratch_shapes=[pltpu.VMEM((Q + SUBLANES, D_XBC), F32), pltpu.VMEM((Q + SUBLANES, D_XBC), F32)],
        compiler_params=_params(("arbitrary", "arbitrary")),
        name="ssd_scan",
    )(xbc, dt, xbc, dt, h0_f, h0_b, lp["ssm_conv_w"], lp["ssm_conv_b"], lp["dt_bias"], lp["a_log"], lp["d_skip"],
      tri3, expand2)


def _layer_norm(v, gain, bias):
    mu = jnp.mean(v, axis=-1, keepdims=True)
    d = v - mu
    var = jnp.mean(d * d, axis=-1, keepdims=True)
    return d * lax.rsqrt(var + LN_EPS) * gain + bias


MIX_ROWS = 64


def _mix_kernel(a_ref, ap_ref, an_ref, up_ref, upp_ref, upn_ref, yf_ref, yb_ref, z_ref, x_ref, gate_ref,
                dww_ref, dwb_ref, clg_ref, clb_ref, pww_ref, pwb_ref, ng_ref, plw_ref, pls_ref, wo_ref,
                l1g_ref, l1b_ref, o_ref, ext_ref, pext_ref, conv_ref, pool_ref, *, tile, seq_len):
    i = pl.program_id(1)
    n_tiles = seq_len // tile
    has_prev = jnp.where(i > 0, 1.0, 0.0)
    has_next = jnp.where(i < n_tiles - 1, 1.0, 0.0)

    def glu(v):
        return v[:, :D_CONV] * jax.nn.sigmoid(v[:, D_CONV:])

    ext_ref[0, 0:HALO, :] = glu(ap_ref[0]) * has_prev
    ext_ref[0, HALO:HALO + tile, :] = glu(a_ref[0])
    ext_ref[0, HALO + tile:, :] = glu(an_ref[0]) * has_next
    pext_ref[0:HALO, :] = upp_ref[0] * has_prev
    pext_ref[HALO:HALO + tile, :] = up_ref[0]
    pext_ref[HALO + tile:, :] = upn_ref[0] * has_next
    n_ext = tile + 2 * HALO

    def shifted(v, k):
        return pltpu.roll(v, (n_ext - k) % n_ext, 0)

    u_ext = ext_ref[0]
    for s in range(1, SUBLANES):
        ext_ref[s] = shifted(u_ext, s)

    assert POOL_WINDOWS == (2, 4, 8, 16)
    p_ext = pext_ref[...]
    sums = [p_ext + shifted(p_ext, -1)]
    for w in POOL_WINDOWS[:-1]:
        sums.append(shifted(sums[-1], -(w // 2)) + shifted(sums[-1], w // 2))
    lane = lax.broadcasted_iota(jnp.int32, (tile, D_POOL), 1)
    group = lane // POOL_GROUP
    back = jnp.zeros((tile, D_POOL), jnp.int32)
    ahead = jnp.zeros((tile, D_POOL), jnp.int32)
    total = sums[-1][HALO:HALO + tile, :]
    for gi, w in enumerate(POOL_WINDOWS):
        back = jnp.where(group == gi, w // 2, back)
        ahead = jnp.where(group == gi, w - w // 2, ahead)
        if gi < len(POOL_WINDOWS) - 1:
            total = jnp.where(group == gi, sums[gi][HALO:HALO + tile, :], total)
    t = lax.broadcasted_iota(jnp.int32, (tile, D_POOL), 0) + i * tile
    cnt = jnp.minimum(t + ahead, seq_len) - jnp.maximum(t - back, 0)
    pool_ref[...] = total / cnt.astype(F32) - p_ext[HALO:HALO + tile, :]

    def tap(ref, o):
        s = o % SUBLANES
        return ref[s, o - s:o - s + MIX_ROWS, :]

    half_c = CONV_K // 2
    for r in range(tile // MIX_ROWS):
        base = r * MIX_ROWS
        acc = dwb_ref[...] + dww_ref[0:1, :] * tap(ext_ref, base + HALO - half_c)
        for k in range(1, CONV_K):
            acc = acc + dww_ref[k:k + 1, :] * tap(ext_ref, base + HALO - half_c + k)
        conv_ref[base:base + MIX_ROWS, :] = acc

    ya = _silu(_layer_norm(conv_ref[...], clg_ref[...], clb_ref[...]))
    ya = _dot(ya.astype(BF16), pww_ref[...]) + pwb_ref[...]
    yc = _dot(pool_ref[...].astype(BF16), plw_ref[...]) * pls_ref[...]
    v = (yf_ref[0] + yb_ref[0]) * _silu(z_ref[0])
    yb = v * lax.rsqrt(jnp.mean(v * v, axis=-1, keepdims=True) + RMS_EPS) * ng_ref[...]
    mix = (_dot(ya.astype(BF16), wo_ref[0:D_CONV, :])
           + _dot(yb.astype(BF16), wo_ref[D_CONV:D_CONV + D_SSM, :])
           + _dot(yc.astype(BF16), wo_ref[D_CONV + D_SSM:, :]))
    o_ref[0] = _layer_norm(ALPHA * x_ref[0] + gate_ref[0] * mix, l1g_ref[...], l1b_ref[...])


def _mixers(a, up, y_f, y_b, z, x, gate, lp, tile):
    B, L, _ = x.shape
    hb = tile // HALO
    n_halo = L // HALO
    tok = lambda w: pl.BlockSpec((1, tile, w), lambda b, i: (b, i, 0))
    prev = lambda w: pl.BlockSpec((1, HALO, w), lambda b, i: (b, jnp.maximum(i * hb - 1, 0), 0))
    nxt = lambda w: pl.BlockSpec((1, HALO, w), lambda b, i: (b, jnp.minimum((i + 1) * hb, n_halo - 1), 0))
    const = lambda shape: pl.BlockSpec(shape, lambda b, i: (0,) * len(shape))
    return pl.pallas_call(
        functools.partial(_mix_kernel, tile=tile, seq_len=L),
        out_shape=jax.ShapeDtypeStruct((B, L, D_MODEL), F32),
        grid=(B, L // tile),
        in_specs=[
            tok(2 * D_CONV), prev(2 * D_CONV), nxt(2 * D_CONV),
            tok(D_POOL), prev(D_POOL), nxt(D_POOL),
            tok(D_SSM), tok(D_SSM), tok(D_SSM), tok(D_MODEL),
            pl.BlockSpec((1, 1, D_MODEL), lambda b, i: (b, 0, 0)),
            const((CONV_K, D_CONV)), const((1, D_CONV)), const((1, D_CONV)), const((1, D_CONV)),
            const((D_CONV, D_CONV)), const((1, D_CONV)), const((1, D_SSM)),
            const((D_POOL, D_POOL)), const((1, D_POOL)), _layer_block(lp["layer"], (D_MODEL, D_MODEL)),
            const((1, D_MODEL)), const((1, D_MODEL)),
        ],
        out_specs=tok(D_MODEL),
        scratch_shapes=[pltpu.VMEM((SUBLANES, tile + 2 * HALO, D_CONV), F32),
                        pltpu.VMEM((tile + 2 * HALO, D_POOL), F32),
                        pltpu.VMEM((tile, D_CONV), F32), pltpu.VMEM((tile, D_POOL), F32)],
        compiler_params=_params(("arbitrary", "arbitrary")),
        name="mixers_outproj_ln1",
    )(a, a, a, up, up, up, y_f, y_b, z, x, gate,
      lp["conv_dw_w"], lp["conv_dw_b"], lp["conv_ln_g"], lp["conv_ln_b"], lp["conv_pw_w"], lp["conv_pw_b"],
      lp["ssm_norm_g"], lp["pool_w"], lp["pool_scale"], lp["w_out"], lp["ln1_g"], lp["ln1_b"])


def _ffn_kernel(*refs, rows, width, has_halo):
    n_x = 3 if has_halo else 1
    x_ref = refs[0]
    (sh_ref, sc_ref, gate_ref, wu_ref, wd_ref, dww_ref, dwb_ref, l2g_ref, l2b_ref,
     o_ref, h_ref, g0a_ref, g0b_ref, vala_ref, valb_ref, gm_ref, gp_ref, act_ref, acc_ref) = refs[n_x:]
    tile = rows * width
    ext_rows = rows + 2 if has_halo else rows
    first = width if has_halo else 0
    i = pl.program_id(1)
    n_tiles = pl.num_programs(1)

    def modulate(v):
        return v * (1.0 + sc_ref[0]) + sh_ref[0]

    h_ref[first:first + tile, :] = modulate(x_ref[0]).astype(BF16)
    if has_halo:
        xp_ref, xn_ref = refs[1], refs[2]
        h_ref[0:width, :] = (modulate(xp_ref[0]) * jnp.where(i > 0, 1.0, 0.0)).astype(BF16)
        h_ref[first + tile:, :] = (modulate(xn_ref[0]) * jnp.where(i < n_tiles - 1, 1.0, 0.0)).astype(BF16)
    tok = lax.broadcasted_iota(jnp.int32, (width, FF_CHUNK), 0)
    not_first = tok != 0
    not_last = tok != width - 1

    def up_project(j, g0_ref, val_ref):
        g0_ref[...] = _dot(h_ref[...], wu_ref[:, D_FF + j * FF_CHUNK:D_FF + (j + 1) * FF_CHUNK])
        val_ref[...] = _dot(h_ref[first:first + tile, :], wu_ref[:, j * FF_CHUNK:(j + 1) * FF_CHUNK])

    def conv_gate(j, g0_ref, val_ref):
        for e in range(ext_rows):
            blk = g0_ref[e * width:(e + 1) * width, :]
            gm_ref[e * width:(e + 1) * width, :] = jnp.where(not_first, pltpu.roll(blk, 1, 0), 0.0)
            gp_ref[e * width:(e + 1) * width, :] = jnp.where(not_last, pltpu.roll(blk, width - 1, 0), 0.0)
        cols = slice(j * FF_CHUNK, (j + 1) * FF_CHUNK)
        tap = lambda k: dww_ref[k:k + 1, cols]
        for r in range(rows):
            conv = dwb_ref[:, cols]
            for dr in range(FFN_K):
                e = r + dr - 1 + (1 if has_halo else 0)
                if e < 0 or e >= ext_rows:
                    continue
                sl = slice(e * width, (e + 1) * width)
                conv = (conv + tap(3 * dr) * gm_ref[sl, :] + tap(3 * dr + 1) * g0_ref[sl, :]
                        + tap(3 * dr + 2) * gp_ref[sl, :])
            out = slice(r * width, (r + 1) * width)
            inner = conv * (GELU_C0 + GELU_C1 * (conv * conv))
            act = (val_ref[out, :] * conv) * (0.5 * jnp.tanh(inner) + 0.5)
            act_ref[out, j * FF_CHUNK:(j + 1) * FF_CHUNK] = act.astype(BF16)

    pieces = [(c0, min(c0 + DOWN_GROUP, N_FF_CHUNKS), n)
              for c0 in range(0, N_FF_CHUNKS, DOWN_GROUP) for n in range(D_MODEL // FF_CHUNK)]

    def down_project(ready, budget):
        while pieces and budget > 0 and pieces[0][1] <= ready:
            c0, c1, n = pieces.pop(0)
            k_rows = slice(c0 * FF_CHUNK, c1 * FF_CHUNK)
            n_cols = slice(n * FF_CHUNK, (n + 1) * FF_CHUNK)
            part = _dot(act_ref[:, k_rows], wd_ref[k_rows, n_cols])
            if c0 == 0:
                acc_ref[:, n_cols] = part
            else:
                acc_ref[:, n_cols] += part
            budget -= 1

    slots = ((g0a_ref, vala_ref), (g0b_ref, valb_ref))
    up_project(0, *slots[0])
    for j in range(N_FF_CHUNKS):
        if j + 1 < N_FF_CHUNKS:
            up_project(j + 1, *slots[(j + 1) % 2])
        down_project(j, DOWN_PIECES_PER_STEP)
        conv_gate(j, *slots[j % 2])
    down_project(N_FF_CHUNKS, len(pieces))
    o_ref[0] = _layer_norm(ALPHA * x_ref[0] + gate_ref[0] * acc_ref[...], l2g_ref[...], l2b_ref[...])


def _conv_ffn(x, shift, scale, gate, lp, rows, width):
    B, L, _ = x.shape
    tile = rows * width
    n_tiles = L // tile
    has_halo = n_tiles * rows > 1
    n_rows = L // width
    ext = (rows + 2) * width if has_halo else tile
    tok = pl.BlockSpec((1, tile, D_MODEL), lambda b, i: (b, i, 0))
    vec = pl.BlockSpec((1, 1, D_MODEL), lambda b, i: (b, 0, 0))
    const = lambda shape: pl.BlockSpec(shape, lambda b, i: (0,) * len(shape), pipeline_mode=pl.Buffered(1))
    x_specs, x_args = [tok], [x]
    if has_halo:
        x_specs += [pl.BlockSpec((1, width, D_MODEL), lambda b, i: (b, jnp.maximum(i * rows - 1, 0), 0)),
                    pl.BlockSpec((1, width, D_MODEL), lambda b, i: (b, jnp.minimum((i + 1) * rows, n_rows - 1), 0))]
        x_args += [x, x]
    return pl.pallas_call(
        functools.partial(_ffn_kernel, rows=rows, width=width, has_halo=has_halo),
        out_shape=jax.ShapeDtypeStruct((B, L, D_MODEL), F32),
        grid=(B, n_tiles),
        in_specs=x_specs + [
            vec, vec, vec,
            _layer_block(lp["layer"], (D_MODEL, 2 * D_FF), pipeline_mode=pl.Buffered(1)),
            _layer_block(lp["layer"], (D_FF, D_MODEL), pipeline_mode=pl.Buffered(1)),
            const((FFN_K * FFN_K, D_FF)), const((1, D_FF)),
            const((1, D_MODEL)), const((1, D_MODEL)),
        ],
        out_specs=tok,
        scratch_shapes=[pltpu.VMEM((ext, D_MODEL), BF16),
                        pltpu.VMEM((ext, FF_CHUNK), F32), pltpu.VMEM((ext, FF_CHUNK), F32),
                        pltpu.VMEM((tile, FF_CHUNK), F32), pltpu.VMEM((tile, FF_CHUNK), F32),
                        pltpu.VMEM((ext, FF_CHUNK), F32), pltpu.VMEM((ext, FF_CHUNK), F32),
                        pltpu.VMEM((tile, D_FF), BF16), pltpu.VMEM((tile, D_MODEL), F32)],
        compiler_params=_params(("arbitrary", "arbitrary")),
        name="conv_ffn_ln2",
    )(*x_args, shift, scale, gate, lp["w_up"], lp["w_down"], lp["ffn_dw_w"], lp["ffn_dw_b"], lp["ln2_g"], lp["ln2_b"])


def _stacked_matmul_weights(w_in, w_out, w_up, w_down):
    s2 = 2 * D_CONV + D_SSM + D_XBC
    s3 = s2 + 2 * HEADS
    w_in_r = jnp.concatenate([w_in[:, :, :s2], w_in[:, :, s3:], w_in[:, :, s2:s3],
                              jnp.zeros((DEPTH, D_MODEL, DT_PAD - 2 * HEADS), F32)], axis=2).astype(BF16)
    return {"w_in": w_in_r, "w_out": w_out.astype(BF16), "w_up": w_up.astype(BF16), "w_down": w_down.astype(BF16)}


def _layer_params(l, stacked, conv_dw_w, conv_dw_b, conv_ln_g, conv_ln_b, conv_pw_w, conv_pw_b, ssm_conv_w, ssm_conv_b,
                  ssm_dt_bias, ssm_A_log, ssm_D, ssm_norm_g, pool_w, pool_scale, ln1_g, ln1_b, ffn_dw_w, ffn_dw_b,
                  ln2_g, ln2_b):
    row = lambda v: v.reshape(1, -1)
    pad_lanes = lambda v: jnp.pad(v.reshape(1, -1), ((0, 0), (0, DT_PAD - 2 * HEADS)))
    pool_bd = jnp.zeros((D_POOL, D_POOL), F32)
    for g in range(len(POOL_WINDOWS)):
        sl = slice(g * POOL_GROUP, (g + 1) * POOL_GROUP)
        pool_bd = pool_bd.at[sl, sl].set(pool_w[l, g])
    return {
        "layer": l, **stacked,
        "conv_dw_w": conv_dw_w[l], "conv_dw_b": row(conv_dw_b[l]), "conv_ln_g": row(conv_ln_g[l]),
        "conv_ln_b": row(conv_ln_b[l]), "conv_pw_w": conv_pw_w[l].astype(BF16), "conv_pw_b": row(conv_pw_b[l]),
        "ssm_conv_w": ssm_conv_w[l], "ssm_conv_b": ssm_conv_b[l].reshape(2, 1, D_XBC),
        "dt_bias": pad_lanes(ssm_dt_bias[l]), "a_log": pad_lanes(ssm_A_log[l]),
        "d_skip": jnp.repeat(ssm_D[l], HEAD_DIM, axis=1).reshape(2, 1, D_SSM),
        "ssm_norm_g": row(ssm_norm_g[l]), "pool_w": pool_bd.astype(BF16), "pool_scale": row(pool_scale[l]),
        "ln1_g": row(ln1_g[l]), "ln1_b": row(ln1_b[l]),
        "ffn_dw_w": ffn_dw_w[l].reshape(FFN_K * FFN_K, D_FF), "ffn_dw_b": row(ffn_dw_b[l]),
        "ln2_g": row(ln2_g[l]), "ln2_b": row(ln2_b[l]),
    }


def _ssd_branch(h, shift, scale, lp, h0_f, h0_b, tile):
    a, z, xbc, up, dt = _in_projection(h, shift, scale, lp, tile)
    y_f, y_b, st_f, st_b = _ssd_scan(xbc, dt, h0_f, h0_b, lp)
    return (a, up, y_f, y_b, z), st_f, st_b


def kernel(x, c, ctx, c_ctx, w_mod, b_mod, w_in, conv_dw_w, conv_dw_b, conv_ln_g, conv_ln_b, conv_pw_w, conv_pw_b,
           ssm_conv_w, ssm_conv_b, ssm_dt_bias, ssm_A_log, ssm_D, ssm_norm_g, pool_w, pool_scale, w_out, ln1_g, ln1_b,
           w_up, ffn_dw_w, ffn_dw_b, w_down, ln2_g, ln2_b):
    B, L, _ = x.shape
    ctx_len = ctx.shape[1]
    assert L % INPROJ_TILE == 0 and L % SSD_BLOCK == 0 and L % (FFN_ROWS * GRID_W) == 0
    assert ctx_len % SSD_CHUNK == 0 and B + 1 <= MOD_ROWS
    cvec = jnp.concatenate([c, c_ctx[None, :], jnp.zeros((MOD_ROWS - B - 1, D_MODEL), F32)], axis=0)
    mods = _modulation(cvec, w_mod, b_mod.reshape(DEPTH, 1, N_MOD * D_MODEL))
    zero_state = jnp.zeros((B, GROUPS, STATE, HEADS_PER_GROUP * HEAD_DIM), F32)
    lat_tile = 8 * GRID_W
    stacked = _stacked_matmul_weights(w_in, w_out, w_up, w_down)
    for l in range(DEPTH):
        last = l == DEPTH - 1
        lp = _layer_params(l, stacked, conv_dw_w, conv_dw_b, conv_ln_g, conv_ln_b, conv_pw_w, conv_pw_b, ssm_conv_w,
                           ssm_conv_b, ssm_dt_bias, ssm_A_log, ssm_D, ssm_norm_g, pool_w, pool_scale, ln1_g, ln1_b,
                           ffn_dw_w, ffn_dw_b, ln2_g, ln2_b)
        mx = [mods[l, :B, k * D_MODEL:(k + 1) * D_MODEL].reshape(B, 1, D_MODEL) for k in range(N_MOD)]
        mc = [jnp.broadcast_to(mods[l, B, k * D_MODEL:(k + 1) * D_MODEL].reshape(1, 1, D_MODEL), (B, 1, D_MODEL))
              for k in range(N_MOD)]
        (a, up, y_f, y_b, z), st_f, st_b = _ssd_branch(ctx, mc[0], mc[1], lp, zero_state, zero_state, ctx_len)
        if not last:
            ctx1 = _mixers(a, up, y_f, y_b, z, ctx, mc[2], lp, ctx_len)
            ctx = _conv_ffn(ctx1, mc[3], mc[4], mc[5], lp, 1, ctx_len)
        (a, up, y_f, y_b, z), _, _ = _ssd_branch(x, mx[0], mx[1], lp, st_f, st_b, INPROJ_TILE)
        x1 = _mixers(a, up, y_f, y_b, z, x, mx[2], lp, lat_tile)
        x = _conv_ffn(x1, mx[3], mx[4], mx[5], lp, FFN_ROWS, GRID_W)
    return x
```

```python
import functools

import numpy as np
import jax
import jax.numpy as jnp
from jax import lax
from jax.experimental import pallas as pl
from jax.experimental.pallas import tpu as pltpu

F32 = jnp.float32
BF16 = jnp.bfloat16

D_MODEL = 1024
DEPTH = 4
GRID_W = 64
D_CONV = 256
CONV_K = 31
D_SSM = 512
HEAD_DIM = 64
HEADS = 8
GROUPS = 2
HEADS_PER_GROUP = HEADS // GROUPS
STATE = 128
SSM_CONV_K = 4
D_XBC = D_SSM + 2 * GROUPS * STATE
D_POOL = 256
POOL_WINDOWS = (2, 4, 8, 16)
POOL_GROUP = D_POOL // len(POOL_WINDOWS)
D_FF = 2816
FFN_K = 3
N_MOD = 6
ALPHA = (2.0 * DEPTH) ** 0.25
LN_EPS = 1e-5
RMS_EPS = 1e-5

LANES = 128
SUBLANES = 8
HALO = 16
DT_PAD = LANES
IN_COLS_PAD = 2 * D_CONV + D_SSM + D_XBC + D_POOL + DT_PAD
SSD_CHUNK = 128
SSD_BLOCK = 1024
INPROJ_TILE = 1024
FF_CHUNK = 256
N_FF_CHUNKS = D_FF // FF_CHUNK
FFN_ROWS = 8
DOWN_GROUP = 2
DOWN_PIECES_PER_STEP = 2
LOG2_E = float(np.log2(np.e))
GELU_C0 = (2.0 / np.pi) ** 0.5
GELU_C1 = GELU_C0 * 0.044715
MOD_ROWS = 16
MOD_TILE = 1024
VMEM_LIMIT = 52 * 1024 * 1024


def _silu(v):
    return v * jax.nn.sigmoid(v)


def _dot(a, b):
    return jnp.dot(a, b, preferred_element_type=F32)


def _params(sem):
    return pltpu.CompilerParams(dimension_semantics=sem, vmem_limit_bytes=VMEM_LIMIT)


def _mod_kernel(c_ref, w_ref, b_ref, o_ref):
    s = _silu(c_ref[...])
    o_ref[0] = _dot(s.astype(BF16), w_ref[0].astype(BF16)) + b_ref[0]


def _modulation(cvec, w_mod, b_mod):
    n_cols = N_MOD * D_MODEL
    return pl.pallas_call(
        _mod_kernel,
        out_shape=jax.ShapeDtypeStruct((DEPTH, MOD_ROWS, n_cols), F32),
        grid=(DEPTH, n_cols // MOD_TILE),
        in_specs=[
            pl.BlockSpec((MOD_ROWS, D_MODEL), lambda l, j: (0, 0)),
            pl.BlockSpec((1, D_MODEL, MOD_TILE), lambda l, j: (l, 0, j)),
            pl.BlockSpec((1, 1, MOD_TILE), lambda l, j: (l, 0, j)),
        ],
        out_specs=pl.BlockSpec((1, MOD_ROWS, MOD_TILE), lambda l, j: (l, 0, j)),
        compiler_params=_params(("arbitrary", "arbitrary")),
        name="adaln_modulation",
    )(cvec, w_mod, b_mod)


_A0, _Z0, _X0, _P0, _T0 = 0, 2 * D_CONV, 2 * D_CONV + D_SSM, 2 * D_CONV + D_SSM + D_XBC, 2 * D_CONV + D_SSM + D_XBC + D_POOL


def _inproj_kernel(x_ref, sh_ref, sc_ref, w_ref, a_ref, z_ref, xbc_ref, up_ref, dt_ref):
    h = (x_ref[0] * (1.0 + sc_ref[0]) + sh_ref[0]).astype(BF16)
    a_ref[0] = _dot(h, w_ref[:, _A0:_Z0])
    z_ref[0] = _dot(h, w_ref[:, _Z0:_X0])
    xbc_ref[0] = _dot(h, w_ref[:, _X0:_P0])
    up_ref[0] = _dot(h, w_ref[:, _P0:_T0])
    dt_ref[0] = _dot(h, w_ref[:, _T0:IN_COLS_PAD])


def _layer_block(layer, shape, **kw):
    return pl.BlockSpec((None,) + shape, lambda b, i: (layer,) + (0,) * len(shape), **kw)


def _in_projection(x, shift, scale, lp, tile):
    B, L, _ = x.shape
    widths = (2 * D_CONV, D_SSM, D_XBC, D_POOL, DT_PAD)
    tok = lambda w: pl.BlockSpec((1, tile, w), lambda b, i: (b, i, 0))
    vec = pl.BlockSpec((1, 1, D_MODEL), lambda b, i: (b, 0, 0))
    return pl.pallas_call(
        _inproj_kernel,
        out_shape=[jax.ShapeDtypeStruct((B, L, w), F32) for w in widths],
        grid=(B, L // tile),
        in_specs=[tok(D_MODEL), vec, vec, _layer_block(lp["layer"], (D_MODEL, IN_COLS_PAD))],
        out_specs=[tok(w) for w in widths],
        compiler_params=_params(("arbitrary", "arbitrary")),
        name="in_projection",
    )(x, shift, scale, lp["w_in"])


def _split_bf16(v, parts):
    out = []
    for _ in range(parts - 1):
        p = v.astype(BF16)
        out.append(p)
        v = v - p.astype(F32)
    out.append(v.astype(BF16))
    return out


def _ssd_kernel(xf_ref, dtf_ref, xb_ref, dtb_ref, h0f_ref, h0b_ref, cw_ref, cb_ref, dtbias_ref, alog_ref, dskip_ref,
                tri_ref, expand_ref, yf_ref, yb_ref, stf_ref, stb_ref, extf_ref, extb_ref):
    @pl.when(pl.program_id(1) == 0)
    def _():
        stf_ref[...] = h0f_ref[...]
        stb_ref[...] = h0b_ref[...]
        extf_ref[...] = jnp.zeros_like(extf_ref)
        extb_ref[...] = jnp.zeros_like(extb_ref)

    n_sub = xf_ref.shape[1] // SSD_CHUNK
    for sub in range(n_sub):
        _ssd_chunk(xf_ref, dtf_ref, cw_ref, cb_ref, dtbias_ref, alog_ref, dskip_ref, tri_ref, expand_ref, yf_ref,
                   stf_ref, extf_ref, sub * SSD_CHUNK, 0)
        _ssd_chunk(xb_ref, dtb_ref, cw_ref, cb_ref, dtbias_ref, alog_ref, dskip_ref, tri_ref, expand_ref, yb_ref,
                   stb_ref, extb_ref, (n_sub - 1 - sub) * SSD_CHUNK, 1)


def _ssd_chunk(xbc_ref, dt_ref, cw_ref, cb_ref, dtb_ref, alog_ref, dskip_ref, tri_ref, expand_ref, y_ref, st_ref,
               ext_ref, r0, direction):
    Q = SSD_CHUNK
    reverse = direction == 1
    rows = slice(r0, r0 + Q)
    xbc = xbc_ref[0, rows, :]
    n_ext = Q + SUBLANES
    w = [cw_ref[direction, k:k + 1, :] for k in range(SSM_CONV_K)]
    if not reverse:
        ext_ref[SUBLANES:n_ext, :] = xbc
        xe = ext_ref[...]
        xn = pltpu.roll(xe, 1, 0)
        near = w[3] * xe + w[2] * xn
        far = pltpu.roll(w[1] * xe + w[0] * xn, 2, 0)
        conv = (near + far)[SUBLANES:n_ext, :] + cb_ref[direction]
        ext_ref[0:SUBLANES, :] = xbc[Q - SUBLANES:Q, :]
    else:
        ext_ref[0:Q, :] = xbc
        xe = ext_ref[...]
        xn = pltpu.roll(xe, n_ext - 1, 0)
        near = w[3] * xe + w[2] * xn
        far = pltpu.roll(w[1] * xe + w[0] * xn, n_ext - 2, 0)
        conv = (near + far)[0:Q, :] + cb_ref[direction]
        ext_ref[Q:n_ext, :] = xbc[0:SUBLANES, :]
    u = _silu(conv)
    xs = u[:, :D_SSM]
    b_bf = u[:, D_SSM:D_SSM + GROUPS * STATE].astype(BF16)
    c_bf = u[:, D_SSM + GROUPS * STATE:].astype(BF16)

    lane = lax.broadcasted_iota(jnp.int32, (1, LANES), 1)
    own = (lane >= direction * HEADS) & (lane < (direction + 1) * HEADS)
    a_head = jnp.where(own, -jnp.exp(alog_ref[...]), 0.0)
    dt = jax.nn.softplus(dt_ref[0, rows, :] + dtb_ref[...])
    a = dt * a_head
    acum = _dot(tri_ref[direction], jnp.concatenate(_split_bf16(a, 3), axis=0)) * LOG2_E
    acum_t = acum.T
    last = Q - 1 if not reverse else 0
    total = acum[last:last + 1, :]
    stacked = jnp.concatenate([dt, jnp.exp2(acum), jnp.exp2(total - acum)], axis=0)
    expanded = _dot(jnp.concatenate(_split_bf16(stacked, 2), axis=1), expand_ref[direction])
    dt_x, decay_in, decay_out = expanded[0:Q], expanded[Q:2 * Q], expanded[2 * Q:3 * Q]

    xdt = xs * dt_x
    xdt_bf = xdt.astype(BF16)
    xdec_bf = (xdt * decay_out).astype(BF16)
    row = lax.broadcasted_iota(jnp.int32, (Q, Q), 0)
    col = lax.broadcasted_iota(jnp.int32, (Q, Q), 1)
    mask = (row >= col) if not reverse else (row <= col)
    gw = HEADS_PER_GROUP * HEAD_DIM
    for g in range(GROUPS):
        cg = c_bf[:, g * STATE:(g + 1) * STATE]
        bg = b_bf[:, g * STATE:(g + 1) * STATE]
        scores = lax.dot_general(cg, bg, (((1,), (1,)), ((), ())), preferred_element_type=F32)
        state = st_ref[0, g]
        y_off = _dot(cg, state.astype(BF16))
        heads = []
        for hh in range(HEADS_PER_GROUP):
            h = g * HEADS_PER_GROUP + hh
            ln = direction * HEADS + h
            seg = jnp.where(mask, jnp.exp2(acum[:, ln:ln + 1] - acum_t[ln:ln + 1, :]), 0.0)
            heads.append(_dot((scores * seg).astype(BF16), xdt_bf[:, h * HEAD_DIM:(h + 1) * HEAD_DIM]))
        y_diag = jnp.concatenate(heads, axis=1)
        sl = slice(g * gw, (g + 1) * gw)
        y_ref[0, rows, sl] = y_diag + y_off * decay_in[:, sl] + dskip_ref[direction, :, sl] * xs[:, sl]
        upd = lax.dot_general(bg, xdec_bf[:, sl], (((0,), (0,)), ((), ())), preferred_element_type=F32)
        st_ref[0, g] = state * decay_in[last:last + 1, sl] + upd


def _ssd_scan(xbc, dt, h0_f, h0_b, lp):
    B, L, _ = xbc.shape
    Q = SSD_CHUNK
    block = min(SSD_BLOCK, L)
    n_blocks = L // block
    tri = np.tril(np.ones((Q, Q), np.float32))
    tri3 = jnp.asarray(np.stack([np.concatenate([t, t, t], axis=1) for t in (tri, tri.T)]), BF16)
    expand = np.zeros((2, LANES, D_SSM), np.float32)
    for d in range(2):
        for h in range(HEADS):
            expand[d, d * HEADS + h, h * HEAD_DIM:(h + 1) * HEAD_DIM] = 1.0
    expand2 = jnp.asarray(np.concatenate([expand, expand], axis=1), BF16)
    gw = HEADS_PER_GROUP * HEAD_DIM
    const = lambda shape: pl.BlockSpec(shape, lambda b, i: (0,) * len(shape))
    fwd = lambda w: pl.BlockSpec((1, block, w), lambda b, i: (b, i, 0))
    bwd = lambda w: pl.BlockSpec((1, block, w), lambda b, i: (b, n_blocks - 1 - i, 0))
    state_spec = pl.BlockSpec((1, GROUPS, STATE, gw), lambda b, i: (b, 0, 0, 0))
    y_shape = jax.ShapeDtypeStruct((B, L, D_SSM), F32)
    st_shape = jax.ShapeDtypeStruct((B, GROUPS, STATE, gw), F32)
    return pl.pallas_call(
        _ssd_kernel,
        out_shape=[y_shape, y_shape, st_shape, st_shape],
        grid=(B, n_blocks),
        in_specs=[
            fwd(D_XBC), fwd(DT_PAD), bwd(D_XBC), bwd(DT_PAD), state_spec, state_spec,
            const((2, SSM_CONV_K, D_XBC)), const((2, 1, D_XBC)), const((1, DT_PAD)), const((1, DT_PAD)),
            const((2, 1, D_SSM)), const((2, Q, 3 * Q)), const((2, 2 * LANES, D_SSM)),
        ],
        out_specs=[fwd(D_SSM), bwd(D_SSM), state_spec, state_spec],
        scratch_shapes=[pltpu.VMEM((Q + SUBLANES, D_XBC), F32), pltpu.VMEM((Q + SUBLANES, D_XBC), F32)],
        compiler_params=_params(("arbitrary", "arbitrary")),
        name="ssd_scan",
    )(xbc, dt, xbc, dt, h0_f, h0_b, lp["ssm_conv_w"], lp["ssm_conv_b"], lp["dt_bias"], lp["a_log"], lp["d_skip"],
      tri3, expand2)


def _layer_norm(v, gain, bias):
    mu = jnp.mean(v, axis=-1, keepdims=True)
    d = v - mu
    var = jnp.mean(d * d, axis=-1, keepdims=True)
    return d * lax.rsqrt(var + LN_EPS) * gain + bias


MIX_ROWS = 128


def _mix_kernel(a_ref, ap_ref, an_ref, up_ref, upp_ref, upn_ref, yf_ref, yb_ref, z_ref, x_ref, gate_ref,
                dww_ref, dwb_ref, clg_ref, clb_ref, pww_ref, pwb_ref, ng_ref, plw_ref, pls_ref, wo_ref,
                l1g_ref, l1b_ref, o_ref, ext_ref, pext_ref, conv_ref, pool_ref, *, tile, seq_len):
    i = pl.program_id(1)
    n_tiles = seq_len // tile
    has_prev = jnp.where(i > 0, 1.0, 0.0)
    has_next = jnp.where(i < n_tiles - 1, 1.0, 0.0)

    def glu(v):
        return v[:, :D_CONV] * jax.nn.sigmoid(v[:, D_CONV:])

    ext_ref[0, 0:HALO, :] = glu(ap_ref[0]) * has_prev
    ext_ref[0, HALO:HALO + tile, :] = glu(a_ref[0])
    ext_ref[0, HALO + tile:, :] = glu(an_ref[0]) * has_next
    pext_ref[0:HALO, :] = upp_ref[0] * has_prev
    pext_ref[HALO:HALO + tile, :] = up_ref[0]
    pext_ref[HALO + tile:, :] = upn_ref[0] * has_next
    n_ext = tile + 2 * HALO

    def shifted(v, k):
        return pltpu.roll(v, (n_ext - k) % n_ext, 0)

    u_ext = ext_ref[0]
    for s in range(1, SUBLANES):
        ext_ref[s] = shifted(u_ext, s)

    assert POOL_WINDOWS == (2, 4, 8, 16)
    p_ext = pext_ref[...]
    sums = [p_ext + shifted(p_ext, -1)]
    for w in POOL_WINDOWS[:-1]:
        sums.append(shifted(sums[-1], -(w // 2)) + shifted(sums[-1], w // 2))
    lane = lax.broadcasted_iota(jnp.int32, (tile, D_POOL), 1)
    group = lane // POOL_GROUP
    back = jnp.zeros((tile, D_POOL), jnp.int32)
    ahead = jnp.zeros((tile, D_POOL), jnp.int32)
    total = sums[-1][HALO:HALO + tile, :]
    for gi, w in enumerate(POOL_WINDOWS):
        back = jnp.where(group == gi, w // 2, back)
        ahead = jnp.where(group == gi, w - w // 2, ahead)
        if gi < len(POOL_WINDOWS) - 1:
            total = jnp.where(group == gi, sums[gi][HALO:HALO + tile, :], total)
    t = lax.broadcasted_iota(jnp.int32, (tile, D_POOL), 0) + i * tile
    cnt = jnp.minimum(t + ahead, seq_len) - jnp.maximum(t - back, 0)
    pool_ref[...] = total / cnt.astype(F32) - p_ext[HALO:HALO + tile, :]

    def tap(ref, o):
        s = o % SUBLANES
        return ref[s, o - s:o - s + MIX_ROWS, :]

    half_c = CONV_K // 2
    for r in range(tile // MIX_ROWS):
        base = r * MIX_ROWS
        acc = dwb_ref[...] + dww_ref[0:1, :] * tap(ext_ref, base + HALO - half_c)
        for k in range(1, CONV_K):
            acc = acc + dww_ref[k:k + 1, :] * tap(ext_ref, base + HALO - half_c + k)
        conv_ref[base:base + MIX_ROWS, :] = acc

    ya = _silu(_layer_norm(conv_ref[...], clg_ref[...], clb_ref[...]))
    ya = _dot(ya.astype(BF16), pww_ref[...]) + pwb_ref[...]
    yc = _dot(pool_ref[...].astype(BF16), plw_ref[...]) * pls_ref[...]
    v = (yf_ref[0] + yb_ref[0]) * _silu(z_ref[0])
    yb = v * lax.rsqrt(jnp.mean(v * v, axis=-1, keepdims=True) + RMS_EPS) * ng_ref[...]
    mix = (_dot(ya.astype(BF16), wo_ref[0:D_CONV, :])
           + _dot(yb.astype(BF16), wo_ref[D_CONV:D_CONV + D_SSM, :])
           + _dot(yc.astype(BF16), wo_ref[D_CONV + D_SSM:, :]))
    o_ref[0] = _layer_norm(ALPHA * x_ref[0] + gate_ref[0] * mix, l1g_ref[...], l1b_ref[...])


def _mixers(a, up, y_f, y_b, z, x, gate, lp, tile):
    B, L, _ = x.shape
    hb = tile // HALO
    n_halo = L // HALO
    tok = lambda w: pl.BlockSpec((1, tile, w), lambda b, i: (b, i, 0))
    prev = lambda w: pl.BlockSpec((1, HALO, w), lambda b, i: (b, jnp.maximum(i * hb - 1, 0), 0))
    nxt = lambda w: pl.BlockSpec((1, HALO, w), lambda b, i: (b, jnp.minimum((i + 1) * hb, n_halo - 1), 0))
    const = lambda shape: pl.BlockSpec(shape, lambda b, i: (0,) * len(shape))
    return pl.pallas_call(
        functools.partial(_mix_kernel, tile=tile, seq_len=L),
        out_shape=jax.ShapeDtypeStruct((B, L, D_MODEL), F32),
        grid=(B, L // tile),
        in_specs=[
            tok(2 * D_CONV), prev(2 * D_CONV), nxt(2 * D_CONV),
            tok(D_POOL), prev(D_POOL), nxt(D_POOL),
            tok(D_SSM), tok(D_SSM), tok(D_SSM), tok(D_MODEL),
            pl.BlockSpec((1, 1, D_MODEL), lambda b, i: (b, 0, 0)),
            const((CONV_K, D_CONV)), const((1, D_CONV)), const((1, D_CONV)), const((1, D_CONV)),
            const((D_CONV, D_CONV)), const((1, D_CONV)), const((1, D_SSM)),
            const((D_POOL, D_POOL)), const((1, D_POOL)), _layer_block(lp["layer"], (D_MODEL, D_MODEL)),
            const((1, D_MODEL)), const((1, D_MODEL)),
        ],
        out_specs=tok(D_MODEL),
        scratch_shapes=[pltpu.VMEM((SUBLANES, tile + 2 * HALO, D_CONV), F32),
                        pltpu.VMEM((tile + 2 * HALO, D_POOL), F32),
                        pltpu.VMEM((tile, D_CONV), F32), pltpu.VMEM((tile, D_POOL), F32)],
        compiler_params=_params(("arbitrary", "arbitrary")),
        name="mixers_outproj_ln1",
    )(a, a, a, up, up, up, y_f, y_b, z, x, gate,
      lp["conv_dw_w"], lp["conv_dw_b"], lp["conv_ln_g"], lp["conv_ln_b"], lp["conv_pw_w"], lp["conv_pw_b"],
      lp["ssm_norm_g"], lp["pool_w"], lp["pool_scale"], lp["w_out"], lp["ln1_g"], lp["ln1_b"])


def _ffn_kernel(*refs, rows, width, has_halo):
    n_x = 3 if has_halo else 1
    x_ref = refs[0]
    (sh_ref, sc_ref, gate_ref, wu_ref, wd_ref, dww_ref, dwb_ref, l2g_ref, l2b_ref,
     o_ref, h_ref, g0a_ref, g0b_ref, vala_ref, valb_ref, gm_ref, gp_ref, act_ref, acc_ref) = refs[n_x:]
    tile = rows * width
    ext_rows = rows + 2 if has_halo else rows
    first = width if has_halo else 0
    i = pl.program_id(1)
    n_tiles = pl.num_programs(1)

    def modulate(v):
        return v * (1.0 + sc_ref[0]) + sh_ref[0]

    h_ref[first:first + tile, :] = modulate(x_ref[0]).astype(BF16)
    if has_halo:
        xp_ref, xn_ref = refs[1], refs[2]
        h_ref[0:width, :] = (modulate(xp_ref[0]) * jnp.where(i > 0, 1.0, 0.0)).astype(BF16)
        h_ref[first + tile:, :] = (modulate(xn_ref[0]) * jnp.where(i < n_tiles - 1, 1.0, 0.0)).astype(BF16)
    tok = lax.broadcasted_iota(jnp.int32, (width, FF_CHUNK), 0)
    not_first = tok != 0
    not_last = tok != width - 1

    def up_project(j, g0_ref, val_ref):
        g0_ref[...] = _dot(h_ref[...], wu_ref[:, D_FF + j * FF_CHUNK:D_FF + (j + 1) * FF_CHUNK])
        val_ref[...] = _dot(h_ref[first:first + tile, :], wu_ref[:, j * FF_CHUNK:(j + 1) * FF_CHUNK])

    def conv_gate(j, g0_ref, val_ref):
        for e in range(ext_rows):
            blk = g0_ref[e * width:(e + 1) * width, :]
            gm_ref[e * width:(e + 1) * width, :] = jnp.where(not_first, pltpu.roll(blk, 1, 0), 0.0)
            gp_ref[e * width:(e + 1) * width, :] = jnp.where(not_last, pltpu.roll(blk, width - 1, 0), 0.0)
        cols = slice(j * FF_CHUNK, (j + 1) * FF_CHUNK)
        tap = lambda k: dww_ref[k:k + 1, cols]
        for r in range(rows):
            conv = dwb_ref[:, cols]
            for dr in range(FFN_K):
                e = r + dr - 1 + (1 if has_halo else 0)
                if e < 0 or e >= ext_rows:
                    continue
                sl = slice(e * width, (e + 1) * width)
                conv = (conv + tap(3 * dr) * gm_ref[sl, :] + tap(3 * dr + 1) * g0_ref[sl, :]
                        + tap(3 * dr + 2) * gp_ref[sl, :])
            out = slice(r * width, (r + 1) * width)
            inner = conv * (GELU_C0 + GELU_C1 * (conv * conv))
            act = (val_ref[out, :] * conv) * (0.5 * jnp.tanh(inner) + 0.5)
            act_ref[out, j * FF_CHUNK:(j + 1) * FF_CHUNK] = act.astype(BF16)

    pieces = [(c0, min(c0 + DOWN_GROUP, N_FF_CHUNKS), n)
              for c0 in range(0, N_FF_CHUNKS, DOWN_GROUP) for n in range(D_MODEL // FF_CHUNK)]

    def down_project(ready, budget):
        while pieces and budget > 0 and pieces[0][1] <= ready:
            c0, c1, n = pieces.pop(0)
            k_rows = slice(c0 * FF_CHUNK, c1 * FF_CHUNK)
            n_cols = slice(n * FF_CHUNK, (n + 1) * FF_CHUNK)
            part = _dot(act_ref[:, k_rows], wd_ref[k_rows, n_cols])
            if c0 == 0:
                acc_ref[:, n_cols] = part
            else:
                acc_ref[:, n_cols] += part
            budget -= 1

    slots = ((g0a_ref, vala_ref), (g0b_ref, valb_ref))
    up_project(0, *slots[0])
    for j in range(N_FF_CHUNKS):
        if j + 1 < N_FF_CHUNKS:
            up_project(j + 1, *slots[(j + 1) % 2])
        down_project(j, DOWN_PIECES_PER_STEP)
        conv_gate(j, *slots[j % 2])
    down_project(N_FF_CHUNKS, len(pieces))
    o_ref[0] = _layer_norm(ALPHA * x_ref[0] + gate_ref[0] * acc_ref[...], l2g_ref[...], l2b_ref[...])


def _conv_ffn(x, shift, scale, gate, lp, rows, width):
    B, L, _ = x.shape
    tile = rows * width
    n_tiles = L // tile
    has_halo = n_tiles * rows > 1
    n_rows = L // width
    ext = (rows + 2) * width if has_halo else tile
    tok = pl.BlockSpec((1, tile, D_MODEL), lambda b, i: (b, i, 0))
    vec = pl.BlockSpec((1, 1, D_MODEL), lambda b, i: (b, 0, 0))
    const = lambda shape: pl.BlockSpec(shape, lambda b, i: (0,) * len(shape), pipeline_mode=pl.Buffered(1))
    x_specs, x_args = [tok], [x]
    if has_halo:
        x_specs += [pl.BlockSpec((1, width, D_MODEL), lambda b, i: (b, jnp.maximum(i * rows - 1, 0), 0)),
                    pl.BlockSpec((1, width, D_MODEL), lambda b, i: (b, jnp.minimum((i + 1) * rows, n_rows - 1), 0))]
        x_args += [x, x]
    return pl.pallas_call(
        functools.partial(_ffn_kernel, rows=rows, width=width, has_halo=has_halo),
        out_shape=jax.ShapeDtypeStruct((B, L, D_MODEL), F32),
        grid=(B, n_tiles),
        in_specs=x_specs + [
            vec, vec, vec,
            _layer_block(lp["layer"], (D_MODEL, 2 * D_FF), pipeline_mode=pl.Buffered(1)),
            _layer_block(lp["layer"], (D_FF, D_MODEL), pipeline_mode=pl.Buffered(1)),
            const((FFN_K * FFN_K, D_FF)), const((1, D_FF)),
            const((1, D_MODEL)), const((1, D_MODEL)),
        ],
        out_specs=tok,
        scratch_shapes=[pltpu.VMEM((ext, D_MODEL), BF16),
                        pltpu.VMEM((ext, FF_CHUNK), F32), pltpu.VMEM((ext, FF_CHUNK), F32),
                        pltpu.VMEM((tile, FF_CHUNK), F32), pltpu.VMEM((tile, FF_CHUNK), F32),
                        pltpu.VMEM((ext, FF_CHUNK), F32), pltpu.VMEM((ext, FF_CHUNK), F32),
                        pltpu.VMEM((tile, D_FF), BF16), pltpu.VMEM((tile, D_MODEL), F32)],
        compiler_params=_params(("arbitrary", "arbitrary")),
        name="conv_ffn_ln2",
    )(*x_args, shift, scale, gate, lp["w_up"], lp["w_down"], lp["ffn_dw_w"], lp["ffn_dw_b"], lp["ln2_g"], lp["ln2_b"])


def _stacked_matmul_weights(w_in, w_out, w_up, w_down):
    s2 = 2 * D_CONV + D_SSM + D_XBC
    s3 = s2 + 2 * HEADS
    w_in_r = jnp.concatenate([w_in[:, :, :s2], w_in[:, :, s3:], w_in[:, :, s2:s3],
                              jnp.zeros((DEPTH, D_MODEL, DT_PAD - 2 * HEADS), F32)], axis=2).astype(BF16)
    return {"w_in": w_in_r, "w_out": w_out.astype(BF16), "w_up": w_up.astype(BF16), "w_down": w_down.astype(BF16)}


def _layer_params(l, stacked, conv_dw_w, conv_dw_b, conv_ln_g, conv_ln_b, conv_pw_w, conv_pw_b, ssm_conv_w, ssm_conv_b,
                  ssm_dt_bias, ssm_A_log, ssm_D, ssm_norm_g, pool_w, pool_scale, ln1_g, ln1_b, ffn_dw_w, ffn_dw_b,
                  ln2_g, ln2_b):
    row = lambda v: v.reshape(1, -1)
    pad_lanes = lambda v: jnp.pad(v.reshape(1, -1), ((0, 0), (0, DT_PAD - 2 * HEADS)))
    pool_bd = jnp.zeros((D_POOL, D_POOL), F32)
    for g in range(len(POOL_WINDOWS)):
        sl = slice(g * POOL_GROUP, (g + 1) * POOL_GROUP)
        pool_bd = pool_bd.at[sl, sl].set(pool_w[l, g])
    return {
        "layer": l, **stacked,
        "conv_dw_w": conv_dw_w[l], "conv_dw_b": row(conv_dw_b[l]), "conv_ln_g": row(conv_ln_g[l]),
        "conv_ln_b": row(conv_ln_b[l]), "conv_pw_w": conv_pw_w[l].astype(BF16), "conv_pw_b": row(conv_pw_b[l]),
        "ssm_conv_w": ssm_conv_w[l], "ssm_conv_b": ssm_conv_b[l].reshape(2, 1, D_XBC),
        "dt_bias": pad_lanes(ssm_dt_bias[l]), "a_log": pad_lanes(ssm_A_log[l]),
        "d_skip": jnp.repeat(ssm_D[l], HEAD_DIM, axis=1).reshape(2, 1, D_SSM),
        "ssm_norm_g": row(ssm_norm_g[l]), "pool_w": pool_bd.astype(BF16), "pool_scale": row(pool_scale[l]),
        "ln1_g": row(ln1_g[l]), "ln1_b": row(ln1_b[l]),
        "ffn_dw_w": ffn_dw_w[l].reshape(FFN_K * FFN_K, D_FF), "ffn_dw_b": row(ffn_dw_b[l]),
        "ln2_g": row(ln2_g[l]), "ln2_b": row(ln2_b[l]),
    }


def _ssd_branch(h, shift, scale, lp, h0_f, h0_b, tile):
    a, z, xbc, up, dt = _in_projection(h, shift, scale, lp, tile)
    y_f, y_b, st_f, st_b = _ssd_scan(xbc, dt, h0_f, h0_b, lp)
    return (a, up, y_f, y_b, z), st_f, st_b


def kernel(x, c, ctx, c_ctx, w_mod, b_mod, w_in, conv_dw_w, conv_dw_b, conv_ln_g, conv_ln_b, conv_pw_w, conv_pw_b,
           ssm_conv_w, ssm_conv_b, ssm_dt_bias, ssm_A_log, ssm_D, ssm_norm_g, pool_w, pool_scale, w_out, ln1_g, ln1_b,
           w_up, ffn_dw_w, ffn_dw_b, w_down, ln2_g, ln2_b):
    B, L, _ = x.shape
    ctx_len = ctx.shape[1]
    assert L % INPROJ_TILE == 0 and L % SSD_BLOCK == 0 and L % (FFN_ROWS * GRID_W) == 0
    assert ctx_len % SSD_CHUNK == 0 and B + 1 <= MOD_ROWS
    cvec = jnp.concatenate([c, c_ctx[None, :], jnp.zeros((MOD_ROWS - B - 1, D_MODEL), F32)], axis=0)
    mods = _modulation(cvec, w_mod, b_mod.reshape(DEPTH, 1, N_MOD * D_MODEL))
    zero_state = jnp.zeros((B, GROUPS, STATE, HEADS_PER_GROUP * HEAD_DIM), F32)
    lat_tile = 8 * GRID_W
    stacked = _stacked_matmul_weights(w_in, w_out, w_up, w_down)
    for l in range(DEPTH):
        last = l == DEPTH - 1
        lp = _layer_params(l, stacked, conv_dw_w, conv_dw_b, conv_ln_g, conv_ln_b, conv_pw_w, conv_pw_b, ssm_conv_w,
                           ssm_conv_b, ssm_dt_bias, ssm_A_log, ssm_D, ssm_norm_g, pool_w, pool_scale, ln1_g, ln1_b,
                           ffn_dw_w, ffn_dw_b, ln2_g, ln2_b)
        mx = [mods[l, :B, k * D_MODEL:(k + 1) * D_MODEL].reshape(B, 1, D_MODEL) for k in range(N_MOD)]
        mc = [jnp.broadcast_to(mods[l, B, k * D_MODEL:(k + 1) * D_MODEL].reshape(1, 1, D_MODEL), (B, 1, D_MODEL))
              for k in range(N_MOD)]
        (a, up, y_f, y_b, z), st_f, st_b = _ssd_branch(ctx, mc[0], mc[1], lp, zero_state, zero_state, ctx_len)
        if not last:
            ctx1 = _mixers(a, up, y_f, y_b, z, ctx, mc[2], lp, ctx_len)
            ctx = _conv_ffn(ctx1, mc[3], mc[4], mc[5], lp, 1, ctx_len)
        (a, up, y_f, y_b, z), _, _ = _ssd_branch(x, mx[0], mx[1], lp, st_f, st_b, INPROJ_TILE)
        x1 = _mixers(a, up, y_f, y_b, z, x, mx[2], lp, lat_tile)
        x = _conv_ffn(x1, mx[3], mx[4], mx[5], lp, FFN_ROWS, GRID_W)
    return x
```
